```python
import math
import jax, jax.numpy as jnp
from jax import lax
import numpy as np

D_MODEL = 2048
BATCH = 2
SEQ = 4096
DEPTH = 1

CHUNK = 64
N_META = 16
SSD_D_INNER = 2048
SSD_HEADDIM = 64
SSD_HEADS = SSD_D_INNER // SSD_HEADDIM
SSD_GROUPS = 8
SSD_STATE = 128
SSD_CONV = 4
SSD_XBC = SSD_D_INNER + 2 * SSD_GROUPS * SSD_STATE
CF_WIDTH = 2048
CF_CONV = 31
PEER_HEADS = 8
PEER_NKEYS = 128
PEER_EXPERTS = PEER_NKEYS * PEER_NKEYS
PEER_DKEY = 256
PEER_TOPK = 16
PEER_BLOCK = 128
OFF_Z = SSD_D_INNER
OFF_XBC = OFF_Z + SSD_XBC
OFF_DT = OFF_XBC + SSD_HEADS
OFF_CF = OFF_DT + 2 * CF_WIDTH
IN_COLS = OFF_CF + 2 * D_MODEL
LN_EPS = 1e-5
DN_ALPHA = (2.0 * DEPTH) ** 0.25
DN_BETA = (8.0 * DEPTH) ** -0.25

kernel_name = "hybrid_ssd_conformer_peer_block"


def layer_norm(x, g, b):
    xf = x.astype(jnp.float32)
    mu = jnp.mean(xf, -1, keepdims=True)
    var = jnp.mean(jnp.square(xf - mu), -1, keepdims=True)
    return ((xf - mu) * lax.rsqrt(var + LN_EPS) * g.astype(jnp.float32) + b.astype(jnp.float32)).astype(x.dtype)


def causal_dwconv(x, w, b):
    k = w.shape[0]
    xp = jnp.pad(x, ((0, 0), (k - 1, 0), (0, 0)))
    y = lax.conv_general_dilated(xp, w.astype(x.dtype)[:, None, :], window_strides=(1,), padding='VALID',
                                 dimension_numbers=('NWC', 'WIO', 'NWC'), feature_group_count=x.shape[-1])
    return y + b.astype(x.dtype)


def gated_rmsnorm(y, z, w):
    yf = (y.astype(jnp.float32) * jax.nn.silu(z.astype(jnp.float32)))
    shp = yf.shape
    yg = yf.reshape(shp[:-1] + (SSD_GROUPS, shp[-1] // SSD_GROUPS))
    yg = yg * lax.rsqrt(jnp.mean(yg * yg, -1, keepdims=True) + LN_EPS)
    return (yg.reshape(shp) * w.astype(jnp.float32)).astype(z.dtype)


def ssd_chunked(xh, dt, a, bm, cm):
    bsz, L, H, P = xh.shape
    G, N = bm.shape[-2], bm.shape[-1]
    R = H // G
    nc = L // CHUNK
    x = (xh * dt[..., None]).reshape(bsz, nc, CHUNK, G, R, P)
    da = (dt * a).reshape(bsz, nc, CHUNK, G, R)
    bm = bm.reshape(bsz, nc, CHUNK, G, N)
    cm = cm.reshape(bsz, nc, CHUNK, G, N)
    cs = jnp.cumsum(da, axis=2)
    mask = jnp.tril(jnp.ones((CHUNK, CHUNK), dtype=bool))[:, :, None, None]
    seg = cs[:, :, :, None] - cs[:, :, None, :]
    decay = jnp.exp(jnp.where(mask, seg, -jnp.inf))
    scores = jnp.einsum('bcign,bcjgn->bcijg', cm, bm)
    y_diag = jnp.einsum('bcijgr,bcjgrp->bcigrp', scores[..., None] * decay, x)
    xd = x * jnp.exp(cs[:, :, -1:] - cs)[..., None]
    states = jnp.einsum('bcjgn,bcjgrp->bcgrpn', bm, xd)
    chunk_decay = jnp.exp(cs[:, :, -1])

    def step(h, inp):
        s, d = inp
        return h * d[..., None, None] + s, h

    h0 = jnp.zeros((bsz, G, R, P, N), dtype=states.dtype)
    _, prev = lax.scan(step, h0, (jnp.swapaxes(states, 0, 1), jnp.swapaxes(chunk_decay, 0, 1)))
    prev = jnp.swapaxes(prev, 0, 1)
    y_off = jnp.einsum('bcign,bcgrpn->bcigrp', cm, prev) * jnp.exp(cs)[..., None]
    return (y_diag + y_off).reshape(bsz, L, H, P)


def mixer(u, w_in, ssd_conv_w, ssd_conv_b, ssd_dt_bias, ssd_a_log, ssd_d, ssd_norm_w, ssd_out,
          cf_dw_w, cf_dw_b, cf_ln_g, cf_ln_b, cf_out, w_o):
    bsz, L, _ = u.shape
    proj = u @ w_in
    z, xbc, dt_raw, cf_in, gate_raw = jnp.split(proj, [OFF_Z, OFF_XBC, OFF_DT, OFF_CF], axis=-1)
    xbc = jax.nn.silu(causal_dwconv(xbc, ssd_conv_w, ssd_conv_b)).astype(jnp.float32)
    xs, bm, cm = jnp.split(xbc, [SSD_D_INNER, SSD_D_INNER + SSD_GROUPS * SSD_STATE], axis=-1)
    xs = xs.reshape(bsz, L, SSD_HEADS, SSD_HEADDIM)
    bm = bm.reshape(bsz, L, SSD_GROUPS, SSD_STATE)
    cm = cm.reshape(bsz, L, SSD_GROUPS, SSD_STATE)
    dt = jax.nn.softplus(dt_raw.astype(jnp.float32) + ssd_dt_bias.astype(jnp.float32))
    a = -jnp.exp(ssd_a_log.astype(jnp.float32))
    pad = (-L) % CHUNK
    padl = lambda t: jnp.pad(t, ((0, 0), (pad, 0)) + ((0, 0),) * (t.ndim - 2))
    y = ssd_chunked(padl(xs), padl(dt), a, padl(bm), padl(cm))[:, pad:]
    y = y + ssd_d.astype(jnp.float32)[:, None] * xs
    y = gated_rmsnorm(y.reshape(bsz, L, SSD_D_INNER), z, ssd_norm_w)
    y_ssd = y @ ssd_out
    c_val, c_gate = jnp.split(cf_in, 2, axis=-1)
    c = c_val * jax.nn.sigmoid(c_gate)
    c = causal_dwconv(c, cf_dw_w, cf_dw_b)
    c = jax.nn.silu(layer_norm(c, cf_ln_g, cf_ln_b))
    y_cf = c @ cf_out
    g_ssd, g_cf = jnp.split(jax.nn.sigmoid(gate_raw), 2, axis=-1)
    return (g_ssd * y_ssd + g_cf * y_cf) @ w_o


def peer(xf, peer_wq, peer_keys, peer_u, peer_v):
    T = xf.shape[0]
    q = (xf @ peer_wq).reshape(T, PEER_HEADS, 2, PEER_DKEY // 2)
    s = jnp.einsum('thsk,shnk->thsn', q, peer_keys).astype(jnp.float32)
    top_s, top_i = lax.top_k(s, PEER_TOPK)
    cand_s = top_s[:, :, 0, :, None] + top_s[:, :, 1, None, :]
    cand_i = top_i[:, :, 0, :, None] * PEER_NKEYS + top_i[:, :, 1, None, :]
    best_s, best_j = lax.top_k(cand_s.reshape(T, PEER_HEADS, PEER_TOPK * PEER_TOPK), PEER_TOPK)
    idx = jnp.take_along_axis(cand_i.reshape(T, PEER_HEADS, PEER_TOPK * PEER_TOPK), best_j, axis=-1)
    gate = jax.nn.softmax(best_s, axis=-1)
    tp = (-T) % PEER_BLOCK
    nb = (T + tp) // PEER_BLOCK
    xb = jnp.pad(xf, ((0, tp), (0, 0))).reshape(nb, PEER_BLOCK, D_MODEL)
    ib = jnp.pad(idx, ((0, tp), (0, 0), (0, 0))).reshape(nb, PEER_BLOCK, PEER_HEADS, PEER_TOPK)
    gb = jnp.pad(gate, ((0, tp), (0, 0), (0, 0))).reshape(nb, PEER_BLOCK, PEER_HEADS, PEER_TOPK)

    def block(args):
        xt, it, gt = args
        ue = peer_u[it]
        act = jax.nn.gelu(jnp.einsum('td,thkd->thk', xt, ue).astype(jnp.float32), approximate=False) * gt
        ve = peer_v[it]
        return jnp.einsum('thk,thkd->td', act.astype(ve.dtype), ve)

    out = lax.map(block, (xb, ib, gb)).reshape(nb * PEER_BLOCK, D_MODEL)
    return out[:T].astype(xf.dtype)


def setup_inputs(seed: int = 0) -> dict:
    key = jax.random.key(seed)
    ks = jax.random.split(key, 26)
    f32 = jnp.float32
    nrm = lambda k, shp, sc: jax.random.normal(k, shp, f32) * sc
    dt0 = jnp.exp(jax.random.uniform(ks[6], (DEPTH, SSD_HEADS), f32) * (math.log(0.1) - math.log(0.001)) + math.log(0.001))
    return {
        "x": nrm(ks[0], (BATCH, SEQ, D_MODEL), 1.0),
        "meta": nrm(ks[1], (N_META, D_MODEL), 1.0),
        "ln0_g": 1.0 + nrm(ks[2], (D_MODEL,), 0.02),
        "ln0_b": nrm(ks[3], (D_MODEL,), 0.02),
        "w_in": nrm(ks[4], (DEPTH, D_MODEL, IN_COLS), D_MODEL ** -0.5),
        "ssd_conv_w": nrm(ks[5], (DEPTH, SSD_CONV, SSD_XBC), SSD_CONV ** -0.5),
        "ssd_conv_b": nrm(ks[7], (DEPTH, SSD_XBC), 0.01),
        "ssd_dt_bias": dt0 + jnp.log(-jnp.expm1(-dt0)),
        "ssd_a_log": jnp.log(jax.random.uniform(ks[8], (DEPTH, SSD_HEADS), f32, 1.0, 16.0)),
        "ssd_d": 1.0 + nrm(ks[9], (DEPTH, SSD_HEADS), 0.1),
        "ssd_norm_w": 1.0 + nrm(ks[10], (DEPTH, SSD_D_INNER), 0.02),
        "ssd_out": nrm(ks[11], (DEPTH, SSD_D_INNER, D_MODEL), SSD_D_INNER ** -0.5),
        "cf_dw_w": nrm(ks[12], (DEPTH, CF_CONV, CF_WIDTH), CF_CONV ** -0.5),
        "cf_dw_b": nrm(ks[13], (DEPTH, CF_WIDTH), 0.01),
        "cf_ln_g": 1.0 + nrm(ks[14], (DEPTH, CF_WIDTH), 0.02),
        "cf_ln_b": nrm(ks[15], (DEPTH, CF_WIDTH), 0.02),
        "cf_out": nrm(ks[16], (DEPTH, CF_WIDTH, D_MODEL), CF_WIDTH ** -0.5),
        "w_o": nrm(ks[17], (DEPTH, D_MODEL, D_MODEL), DN_BETA * D_MODEL ** -0.5),
        "ln1_g": 1.0 + nrm(ks[18], (DEPTH, D_MODEL), 0.02),
        "ln1_b": nrm(ks[19], (DEPTH, D_MODEL), 0.02),
        "peer_wq": nrm(ks[20], (DEPTH, D_MODEL, PEER_HEADS * PEER_DKEY), D_MODEL ** -0.5),
        "peer_keys": nrm(ks[21], (DEPTH, 2, PEER_HEADS, PEER_NKEYS, PEER_DKEY // 2), (PEER_DKEY // 2) ** -0.5),
        "peer_u": nrm(ks[22], (DEPTH, PEER_EXPERTS, D_MODEL), D_MODEL ** -0.5),
        "peer_v": nrm(ks[23], (DEPTH, PEER_EXPERTS, D_MODEL), DN_BETA),
        "ln2_g": 1.0 + nrm(ks[24], (DEPTH, D_MODEL), 0.02),
        "ln2_b": nrm(ks[25], (DEPTH, D_MODEL), 0.02),
    }


def reference(x, meta, ln0_g, ln0_b, w_in, ssd_conv_w, ssd_conv_b, ssd_dt_bias, ssd_a_log, ssd_d,
              ssd_norm_w, ssd_out, cf_dw_w, cf_dw_b, cf_ln_g, cf_ln_b, cf_out, w_o, ln1_g, ln1_b,
              peer_wq, peer_keys, peer_u, peer_v, ln2_g, ln2_b):
    bsz = x.shape[0]
    h = jnp.concatenate([jnp.broadcast_to(meta[None].astype(x.dtype), (bsz, N_META, D_MODEL)), x], axis=1)
    h = layer_norm(h, ln0_g, ln0_b)
    for l in range(DEPTH):
        m = mixer(h, w_in[l], ssd_conv_w[l], ssd_conv_b[l], ssd_dt_bias[l], ssd_a_log[l], ssd_d[l],
                  ssd_norm_w[l], ssd_out[l], cf_dw_w[l], cf_dw_b[l], cf_ln_g[l], cf_ln_b[l], cf_out[l], w_o[l])
        h = layer_norm(DN_ALPHA * h + m, ln1_g[l], ln1_b[l])
        f = peer(h.reshape(-1, D_MODEL), peer_wq[l], peer_keys[l], peer_u[l], peer_v[l]).reshape(h.shape)
        h = layer_norm(DN_ALPHA * h + f, ln2_g[l], ln2_b[l])
    return h[:, N_META:]
```

```python
import functools
import math

import jax
import jax.numpy as jnp
from jax import lax
from jax.experimental import pallas as pl
from jax.experimental.pallas import tpu as pltpu

F32 = jnp.float32
BF16 = jnp.bfloat16

D_MODEL = 2048
CHUNK = 64
N_META = 16
PAD_ROWS = CHUNK - N_META
SSD_D_INNER = 2048
SSD_HEADDIM = 64
SSD_HEADS = 32
SSD_GROUPS = 8
SSD_STATE = 128
SSD_CONV = 4
SSD_XBC = SSD_D_INNER + 2 * SSD_GROUPS * SSD_STATE
GROUP_COLS = SSD_D_INNER // SSD_GROUPS
HEADS_PER_GROUP = SSD_HEADS // SSD_GROUPS
CF_WIDTH = 2048
CF_CONV = 31
CF_HALO = 32
PEER_HEADS = 8
PEER_NKEYS = 128
PEER_EXPERTS = PEER_NKEYS * PEER_NKEYS
PEER_DKEY = 256
PEER_TOPK = 16
OFF_Z = SSD_D_INNER
OFF_XBC = OFF_Z + SSD_XBC
OFF_DT = OFF_XBC + SSD_HEADS
OFF_CF = OFF_DT + 2 * CF_WIDTH
LN_EPS = 1e-5
DN_ALPHA = 2.0 ** 0.25
LANES = 128
VMEM_LIMIT = 56 * 1024 * 1024

_CAND = [(a, b) for a in range(PEER_TOPK) for b in range(PEER_TOPK)
         if (a + 1) * (b + 1) <= PEER_TOPK]


def _cparams(sem):
    return pltpu.CompilerParams(dimension_semantics=sem, vmem_limit_bytes=VMEM_LIMIT)


def _resident(shape):
    nd = len(shape)
    return pl.BlockSpec(shape, lambda *_: (0,) * nd, pipeline_mode=pl.Buffered(1))


def _divisor_tile(n, target, mult):
    best = None
    for t in range(mult, min(n, target) + 1, mult):
        if n % t == 0:
            best = t
    assert best is not None, (n, target, mult)
    return best


def _layer_norm(x, g, b):
    mu = jnp.mean(x, axis=-1, keepdims=True)
    xc = x - mu
    var = jnp.mean(xc * xc, axis=-1, keepdims=True)
    return xc * lax.rsqrt(var + LN_EPS) * g + b


def _silu(x):
    return x * jax.nn.sigmoid(x)


def _ln_kernel(x_ref, g_ref, b_ref, o_ref, ob_ref):
    y = _layer_norm(x_ref[...], g_ref[...], b_ref[...])
    o_ref[...] = y
    ob_ref[...] = y.astype(BF16)


def _ln_call(x, g, b, rows):
    tp, d = x.shape
    return pl.pallas_call(
        _ln_kernel,
        grid=(tp // rows,),
        in_specs=[pl.BlockSpec((rows, d), lambda i: (i, 0)),
                  pl.BlockSpec((1, d), lambda i: (0, 0)),
                  pl.BlockSpec((1, d), lambda i: (0, 0))],
        out_specs=[pl.BlockSpec((rows, d), lambda i: (i, 0)),
                   pl.BlockSpec((rows, d), lambda i: (i, 0))],
        out_shape=[jax.ShapeDtypeStruct((tp, d), F32), jax.ShapeDtypeStruct((tp, d), BF16)],
        compiler_params=_cparams(("parallel",)),
        name="ln0",
    )(x, g.reshape(1, d), b.reshape(1, d))


def _proj_kernel(x_ref, *refs, n_w, epilogue, has_bias):
    w_refs = refs[:n_w]
    b_ref = refs[n_w] if has_bias else None
    o_ref = refs[-1]
    x = x_ref[...]
    accs = [jnp.dot(x, w[...], preferred_element_type=F32) for w in w_refs]
    if has_bias:
        accs[0] = accs[0] + b_ref[...]
    o_ref[...] = epilogue(*accs).astype(o_ref.dtype)


def _proj_call(x, ws, epilogue, out_dtype, tm, tn, bias=None, name="proj"):
    tp, k = x.shape
    n = ws[0].shape[1]
    in_specs = [pl.BlockSpec((tm, k), lambda i, j: (i, 0))]
    in_specs += [pl.BlockSpec((k, tn), lambda i, j: (0, j)) for _ in ws]
    args = [x, *ws]
    if bias is not None:
        in_specs.append(pl.BlockSpec((1, tn), lambda i, j: (0, j)))
        args.append(bias)
    return pl.pallas_call(
        functools.partial(_proj_kernel, n_w=len(ws), epilogue=epilogue, has_bias=bias is not None),
        grid=(tp // tm, n // tn),
        in_specs=in_specs,
        out_specs=pl.BlockSpec((tm, tn), lambda i, j: (i, j)),
        out_shape=jax.ShapeDtypeStruct((tp, n), out_dtype),
        compiler_params=_cparams(("parallel", "parallel")),
        name=name,
    )(*args)


def _softplus(x):
    return jnp.maximum(x, 0.0) + jnp.log1p(jnp.exp(-jnp.abs(x)))


def _ssd_kernel(xbc_ref, zs_ref, dt_ref, cw_ref, cb_ref, alog_ref, dskip_ref, nw_ref, hexp_ref,
                o_ref, ext_ref, state_ref, y_ref, *, rows):
    i = pl.program_id(1)

    @pl.when(i == 0)
    def _():
        ext_ref[0:8, :] = jnp.zeros((8, SSD_XBC), F32)
        state_ref[...] = jnp.zeros(state_ref.shape, F32)

    row = i * rows + lax.broadcasted_iota(jnp.int32, (rows, 1), 0)
    valid = row >= PAD_ROWS
    ext_ref[8:8 + rows, :] = jnp.where(valid, xbc_ref[...].astype(F32), 0.0)
    acc = cb_ref[...] + cw_ref[0:1, :] * ext_ref[pl.ds(8 - (SSD_CONV - 1), rows), :]
    for k in range(1, SSD_CONV):
        acc = acc + cw_ref[k:k + 1, :] * ext_ref[pl.ds(8 - (SSD_CONV - 1) + k, rows), :]
    ext_ref[0:8, :] = ext_ref[rows:rows + 8, :]
    xc = _silu(acc)
    xs = xc[:, :SSD_D_INNER]
    bm = xc[:, SSD_D_INNER:SSD_D_INNER + SSD_GROUPS * SSD_STATE].astype(BF16)
    cm = xc[:, SSD_D_INNER + SSD_GROUPS * SSD_STATE:].astype(BF16)

    dt = jnp.where(valid, dt_ref[...], 0.0)
    da = dt * (-jnp.exp(alog_ref[...]))
    pos = lax.broadcasted_iota(jnp.int32, (rows, 1), 0) % CHUNK
    cs = da
    shift = 1
    while shift < CHUNK:
        cs = cs + jnp.where(pos >= shift, pltpu.roll(cs, shift, axis=0), 0.0)
        shift *= 2
    hexp = hexp_ref[...]
    dt_e = jnp.dot(dt, hexp, preferred_element_type=F32, precision=lax.Precision.HIGHEST)
    cs_e = jnp.dot(cs, hexp, preferred_element_type=F32, precision=lax.Precision.HIGHEST)
    xdt = xs * dt_e

    sub = lax.broadcasted_iota(jnp.int32, (CHUNK, SSD_D_INNER), 0)
    lane_pos = lax.broadcasted_iota(jnp.int32, (CHUNK, SSD_D_INNER), 1) % SSD_HEADDIM
    diag = sub == lane_pos
    causal = sub >= lane_pos
    blk_r = lax.broadcasted_iota(jnp.int32, (GROUP_COLS, GROUP_COLS), 0) // SSD_HEADDIM
    blk_c = lax.broadcasted_iota(jnp.int32, (GROUP_COLS, GROUP_COLS), 1) // SSD_HEADDIM
    same_head = blk_r == blk_c

    for c in range(rows // CHUNK):
        sl = slice(c * CHUNK, (c + 1) * CHUNK)
        cs_c = cs_e[sl]
        cs_row = jnp.sum(jnp.where(diag, cs_c, 0.0), axis=0, keepdims=True)
        cs_end = cs_c[CHUNK - 1:CHUNK, :]
        decay_l = jnp.where(causal, jnp.exp(cs_c - cs_row), 0.0)
        decay_in = jnp.exp(cs_c)
        decay_out = jnp.exp(cs_end - cs_c)
        decay_chunk = jnp.exp(cs_end)
        for g in range(SSD_GROUPS):
            gc = slice(g * GROUP_COLS, (g + 1) * GROUP_COLS)
            gn = slice(g * SSD_STATE, (g + 1) * SSD_STATE)
            cm_g = cm[sl, gn]
            bm_g = bm[sl, gn]
            bm_rep = jnp.concatenate([bm_g] * HEADS_PER_GROUP, axis=0)
            scores = lax.dot_general(cm_g, bm_rep, (((1,), (1,)), ((), ())),
                                     preferred_element_type=F32)
            m = (scores * decay_l[:, gc]).astype(BF16)
            x_g = xdt[sl, gc]
            x_rep = jnp.concatenate([x_g] * HEADS_PER_GROUP, axis=0)
            x_bd = jnp.where(same_head, x_rep, 0.0).astype(BF16)
            y_diag = jnp.dot(m, x_bd, preferred_element_type=F32)
            st = state_ref[g]
            y_off = jnp.dot(cm_g, st.astype(BF16), preferred_element_type=F32) * decay_in[:, gc]
            y_ref[sl, gc] = y_diag + y_off
            xd = (x_g * decay_out[:, gc]).astype(BF16)
            upd = lax.dot_general(bm_g, xd, (((0,), (0,)), ((), ())), preferred_element_type=F32)
            state_ref[g] = st * decay_chunk[:, gc] + upd

    y = y_ref[...] + dskip_ref[...] * xs
    yf = y * zs_ref[...].astype(F32)
    for g in range(SSD_GROUPS):
        gc = slice(g * GROUP_COLS, (g + 1) * GROUP_COLS)
        seg = yf[:, gc]
        ms = jnp.mean(seg * seg, axis=-1, keepdims=True)
        o_ref[:, gc] = (seg * lax.rsqrt(ms + LN_EPS) * nw_ref[:, gc]).astype(o_ref.dtype)


def _ssd_call(xbc, zs, dt, cw, cb, alog, dskip, nw, hexp, bsz, lp, rows):
    nblk = lp // rows
    rmap = lambda b, i: (b * nblk + i, 0)
    cmap = lambda b, i: (0, 0)
    return pl.pallas_call(
        functools.partial(_ssd_kernel, rows=rows),
        grid=(bsz, nblk),
        in_specs=[pl.BlockSpec((rows, SSD_XBC), rmap),
                  pl.BlockSpec((rows, SSD_D_INNER), rmap),
                  pl.BlockSpec((rows, LANES), rmap),
                  pl.BlockSpec((SSD_CONV, SSD_XBC), cmap),
                  pl.BlockSpec((1, SSD_XBC), cmap),
                  pl.BlockSpec((1, LANES), cmap),
                  pl.BlockSpec((1, SSD_D_INNER), cmap),
                  pl.BlockSpec((1, SSD_D_INNER), cmap),
                  pl.BlockSpec((LANES, SSD_D_INNER), cmap)],
        out_specs=pl.BlockSpec((rows, SSD_D_INNER), rmap),
        out_shape=jax.ShapeDtypeStruct((bsz * lp, SSD_D_INNER), BF16),
        scratch_shapes=[pltpu.VMEM((8 + rows, SSD_XBC), F32),
                        pltpu.VMEM((SSD_GROUPS, SSD_STATE, GROUP_COLS), F32),
                        pltpu.VMEM((rows, SSD_D_INNER), F32)],
        compiler_params=_cparams(("arbitrary", "arbitrary")),
        name="ssd",
    )(xbc, zs, dt, cw, cb, alog, dskip, nw, hexp)


def _conf_kernel(c_ref, w_ref, b_ref, g_ref, beta_ref, o_ref, ext_ref, *, rows):
    i = pl.program_id(1)

    @pl.when(i == 0)
    def _():
        ext_ref[0:CF_HALO, :] = jnp.zeros((CF_HALO, CF_WIDTH), F32)

    row = i * rows + lax.broadcasted_iota(jnp.int32, (rows, 1), 0)
    ext_ref[CF_HALO:CF_HALO + rows, :] = jnp.where(row >= PAD_ROWS, c_ref[...].astype(F32), 0.0)
    base = CF_HALO - (CF_CONV - 1)
    acc = b_ref[...] + w_ref[0:1, :] * ext_ref[pl.ds(base, rows), :]
    for k in range(1, CF_CONV):
        acc = acc + w_ref[k:k + 1, :] * ext_ref[pl.ds(base + k, rows), :]
    ext_ref[0:CF_HALO, :] = ext_ref[rows:rows + CF_HALO, :]
    o_ref[...] = _silu(_layer_norm(acc, g_ref[...], beta_ref[...])).astype(o_ref.dtype)


def _conf_call(c, w, b, g, beta, bsz, lp, rows):
    nblk = lp // rows
    rmap = lambda bb, i: (bb * nblk + i, 0)
    cmap = lambda bb, i: (0, 0)
    return pl.pallas_call(
        functools.partial(_conf_kernel, rows=rows),
        grid=(bsz, nblk),
        in_specs=[pl.BlockSpec((rows, CF_WIDTH), rmap),
                  pl.BlockSpec((CF_CONV, CF_WIDTH), cmap),
                  pl.BlockSpec((1, CF_WIDTH), cmap),
                  pl.BlockSpec((1, CF_WIDTH), cmap),
                  pl.BlockSpec((1, CF_WIDTH), cmap)],
        out_specs=pl.BlockSpec((rows, CF_WIDTH), rmap),
        out_shape=jax.ShapeDtypeStruct((bsz * lp, CF_WIDTH), BF16),
        scratch_shapes=[pltpu.VMEM((CF_HALO + rows, CF_WIDTH), F32)],
        compiler_params=_cparams(("arbitrary", "arbitrary")),
        name="conformer",
    )(c, w, b, g, beta)


def _merge_kernel(yn_ref, c_ref, gate_ref, h_ref, wssd_ref, wcf_ref, wo_ref, g_ref, b_ref,
                  o_ref, ob_ref):
    y_ssd = jnp.dot(yn_ref[...], wssd_ref[...], preferred_element_type=F32)
    y_cf = jnp.dot(c_ref[...], wcf_ref[...], preferred_element_type=F32)
    gates = gate_ref[...].astype(F32)
    mix = gates[:, :D_MODEL] * y_ssd + gates[:, D_MODEL:] * y_cf
    m = jnp.dot(mix.astype(BF16), wo_ref[...], preferred_element_type=F32)
    h1 = _layer_norm(DN_ALPHA * h_ref[...] + m, g_ref[...], b_ref[...])
    o_ref[...] = h1
    ob_ref[...] = h1.astype(BF16)


def _merge_call(yn, c, gates, h0, wssd, wcf, wo, g, b, tm):
    tp = yn.shape[0]
    rmap = lambda i: (i, 0)
    return pl.pallas_call(
        _merge_kernel,
        grid=(tp // tm,),
        in_specs=[pl.BlockSpec((tm, SSD_D_INNER), rmap),
                  pl.BlockSpec((tm, CF_WIDTH), rmap),
                  pl.BlockSpec((tm, 2 * D_MODEL), rmap),
                  pl.BlockSpec((tm, D_MODEL), rmap),
                  _resident((SSD_D_INNER, D_MODEL)),
                  _resident((CF_WIDTH, D_MODEL)),
                  _resident((D_MODEL, D_MODEL)),
                  _resident((1, D_MODEL)),
                  _resident((1, D_MODEL))],
        out_specs=[pl.BlockSpec((tm, D_MODEL), rmap), pl.BlockSpec((tm, D_MODEL), rmap)],
        out_shape=[jax.ShapeDtypeStruct((tp, D_MODEL), F32),
                   jax.ShapeDtypeStruct((tp, D_MODEL), BF16)],
        compiler_params=_cparams(("parallel",)),
        name="merge",
    )(yn, c, gates, h0, wssd, wcf, wo, g, b)


def _top16_major(s_ref, val_ref, idx_ref, n):
    neg = jnp.float32(-jnp.inf)
    m0 = s_ref[0]
    for j in range(1, n):
        m0 = jnp.maximum(m0, s_ref[j])

    def body(it, m):
        idx = jnp.full(m.shape, n, jnp.int32)
        for j in range(n):
            idx = jnp.minimum(idx, jnp.where(s_ref[j] == m, j, n))
        val_ref[it] = m
        idx_ref[it] = idx
        nxt = jnp.full(m.shape, neg, F32)
        for j in range(n):
            sj = jnp.where(idx == j, neg, s_ref[j])
            s_ref[j] = sj
            nxt = jnp.maximum(nxt, sj)
        return nxt

    lax.fori_loop(0, PEER_TOPK, body, m0)


def _peer_query_kernel(hb_ref, wq_ref, kbig_ref, i1_ref, i2_ref, gate_ref,
                       s_ref, v1_ref, x1_ref, v2_ref, x2_ref, cand_ref, bs_ref, e1_ref, e2_ref):
    tq = hb_ref.shape[0]
    q = jnp.dot(hb_ref[...], wq_ref[...], preferred_element_type=F32).astype(BF16)
    for half, (v_ref, x_ref) in enumerate(((v1_ref, x1_ref), (v2_ref, x2_ref))):
        st = lax.dot_general(kbig_ref[half], q, (((1,), (1,)), ((), ())),
                             preferred_element_type=F32)
        s_ref[...] = st.reshape(PEER_NKEYS, PEER_HEADS, tq)
        _top16_major(s_ref, v_ref, x_ref, PEER_NKEYS)

    ncand = len(_CAND)
    for ci, (a, b) in enumerate(_CAND):
        cand_ref[ci] = v1_ref[a] + v2_ref[b]
    neg = jnp.float32(-jnp.inf)
    m0 = cand_ref[0]
    for ci in range(1, ncand):
        m0 = jnp.maximum(m0, cand_ref[ci])
    big = PEER_TOPK * PEER_TOPK

    def body(it, m):
        sel = jnp.full(m.shape, big, jnp.int32)
        for ci, (a, b) in enumerate(_CAND):
            sel = jnp.minimum(sel, jnp.where(cand_ref[ci] == m, a * PEER_TOPK + b, big))
        e1 = jnp.zeros(m.shape, jnp.int32)
        e2 = jnp.zeros(m.shape, jnp.int32)
        nxt = jnp.full(m.shape, neg, F32)
        for ci, (a, b) in enumerate(_CAND):
            hit = sel == a * PEER_TOPK + b
            e1 = jnp.where(hit, x1_ref[a], e1)
            e2 = jnp.where(hit, x2_ref[b], e2)
            cj = jnp.where(hit, neg, cand_ref[ci])
            cand_ref[ci] = cj
            nxt = jnp.maximum(nxt, cj)
        bs_ref[it] = m
        e1_ref[it] = e1
        e2_ref[it] = e2
        return nxt

    lax.fori_loop(0, PEER_TOPK, body, m0)

    bs = bs_ref[...]
    ex = jnp.exp(bs - bs[0:1])
    gate = ex / jnp.sum(ex, axis=0, keepdims=True)
    nj = PEER_TOPK * PEER_HEADS
    gate_ref[...] = gate.reshape(nj, tq).T
    i1_ref[...] = e1_ref[...].reshape(nj, tq).T
    i2_ref[...] = e2_ref[...].reshape(nj, tq).T


def _peer_query_call(hb, wq, kbig, tq):
    tp = hb.shape[0]
    nj = PEER_TOPK * PEER_HEADS
    rmap = lambda i: (i, 0)
    slab = lambda n, dt: pltpu.VMEM((n, PEER_HEADS, tq), dt)
    return pl.pallas_call(
        _peer_query_kernel,
        grid=(tp // tq,),
        in_specs=[pl.BlockSpec((tq, D_MODEL), rmap),
                  _resident((D_MODEL, PEER_HEADS * PEER_DKEY)),
                  _resident((2, PEER_NKEYS * PEER_HEADS, PEER_HEADS * PEER_DKEY))],
        out_specs=[pl.BlockSpec((tq, nj), rmap)] * 3,
        out_shape=[jax.ShapeDtypeStruct((tp, nj), jnp.int32),
                   jax.ShapeDtypeStruct((tp, nj), jnp.int32),
                   jax.ShapeDtypeStruct((tp, nj), F32)],
        scratch_shapes=[slab(PEER_NKEYS, F32),
                        slab(PEER_TOPK, F32), slab(PEER_TOPK, jnp.int32),
                        slab(PEER_TOPK, F32), slab(PEER_TOPK, jnp.int32),
                        slab(len(_CAND), F32), slab(PEER_TOPK, F32),
                        slab(PEER_TOPK, jnp.int32), slab(PEER_TOPK, jnp.int32)],
        compiler_params=_cparams(("parallel",)),
        name="peer_query",
    )(hb, wq, kbig)


def _gelu(x):
    return 0.5 * x * (1.0 + lax.erf(x * (1.0 / math.sqrt(2.0))))


def _peer_expert_kernel(hb_ref, h_ref, i1_ref, i2_ref, gate_ref, u_ref, v_ref, g_ref, b_ref,
                        o_ref, gs_ref, acc_ref, *, tm, te, stride):
    e = pl.program_id(1)

    @pl.when(e == 0)
    def _():
        acc_ref[...] = jnp.zeros(acc_ref.shape, F32)
        key_iota = lax.broadcasted_iota(jnp.int32, (PEER_NKEYS, PEER_NKEYS), 0)

        def build(t, carry):
            r1 = i1_ref[pl.ds(t, 1), :]
            r2 = i2_ref[pl.ds(t, 1), :]
            gt = gate_ref[pl.ds(t, 1), :]
            a_t = jnp.where(key_iota == r1, gt, 0.0).astype(BF16)
            b_t = jnp.where(key_iota == r2, 1.0, 0.0).astype(BF16)
            g_t = lax.dot_general(a_t, b_t, (((1,), (1,)), ((), ())), preferred_element_type=F32)
            gs_ref[pl.ds(t, PEER_NKEYS, stride=stride), :] = g_t
            return carry

        lax.fori_loop(0, tm, build, 0)

    s = lax.dot_general(hb_ref[...], u_ref[...], (((1,), (1,)), ((), ())),
                        preferred_element_type=F32)
    slabs = te // PEER_NKEYS
    route = jnp.concatenate(
        [gs_ref[pl.ds(pl.multiple_of((e * slabs + r) * stride, 8), tm), :] for r in range(slabs)],
        axis=1)
    act = (_gelu(s) * route).astype(BF16)
    acc_ref[...] += jnp.dot(act, v_ref[...], preferred_element_type=F32)

    @pl.when(e == pl.num_programs(1) - 1)
    def _():
        o_ref[...] = _layer_norm(DN_ALPHA * h_ref[...] + acc_ref[...], g_ref[...], b_ref[...])


def _peer_expert_call(hb, h, i1, i2, gate, u, v, g, b, tm, te):
    tp = hb.shape[0]
    nj = PEER_TOPK * PEER_HEADS
    stride = tm + 8
    rmap = lambda i, e: (i, 0)
    emap = lambda i, e: (e, 0)
    cmap = lambda i, e: (0, 0)
    return pl.pallas_call(
        functools.partial(_peer_expert_kernel, tm=tm, te=te, stride=stride),
        grid=(tp // tm, PEER_EXPERTS // te),
        in_specs=[pl.BlockSpec((tm, D_MODEL), rmap),
                  pl.BlockSpec((tm, D_MODEL), rmap),
                  pl.BlockSpec((tm, nj), rmap),
                  pl.BlockSpec((tm, nj), rmap),
                  pl.BlockSpec((tm, nj), rmap),
                  pl.BlockSpec((te, D_MODEL), emap),
                  pl.BlockSpec((te, D_MODEL), emap),
                  pl.BlockSpec((1, D_MODEL), cmap),
                  pl.BlockSpec((1, D_MODEL), cmap)],
        out_specs=pl.BlockSpec((tm, D_MODEL), rmap),
        out_shape=jax.ShapeDtypeStruct((tp, D_MODEL), F32),
        scratch_shapes=[pltpu.VMEM((PEER_NKEYS * stride, PEER_NKEYS), F32),
                        pltpu.VMEM((tm, D_MODEL), F32)],
        compiler_params=_cparams(("parallel", "arbitrary")),
        name="peer_experts",
    )(hb, h, i1, i2, gate, u, v, g, b)


def kernel(x, meta, ln0_g, ln0_b, w_in, ssd_conv_w, ssd_conv_b, ssd_dt_bias, ssd_a_log, ssd_d,
           ssd_norm_w, ssd_out, cf_dw_w, cf_dw_b, cf_ln_g, cf_ln_b, cf_out, w_o, ln1_g, ln1_b,
           peer_wq, peer_keys, peer_u, peer_v, ln2_g, ln2_b):
    bsz, seq, d = x.shape
    assert d == D_MODEL and seq % CHUNK == 0
    assert w_in.shape[0] == 1, "single layer"
    lp = PAD_ROWS + N_META + seq
    tp = bsz * lp
    assert tp % LANES == 0

    rows = _divisor_tile(lp, 320, CHUNK)
    tm_proj = _divisor_tile(tp, 1664, 8)
    tm_merge = _divisor_tile(tp, 320, 8)
    tq = _divisor_tile(tp, 640, LANES)
    tm_peer = _divisor_tile(tp, 320, 32)

    hcat = jnp.concatenate(
        [jnp.zeros((bsz, PAD_ROWS, d), x.dtype),
         jnp.broadcast_to(meta[None].astype(x.dtype), (bsz, N_META, d)), x], axis=1).reshape(tp, d)
    h0, h0b = _ln_call(hcat, ln0_g, ln0_b, _divisor_tile(tp, 320, 8))

    w = w_in[0]
    wz = w[:, :OFF_Z].astype(BF16)
    wxbc = w[:, OFF_Z:OFF_XBC].astype(BF16)
    wdt = jnp.pad(w[:, OFF_XBC:OFF_DT], ((0, 0), (0, LANES - SSD_HEADS))).astype(BF16)
    wcv = w[:, OFF_DT:OFF_DT + CF_WIDTH].astype(BF16)
    wcg = w[:, OFF_DT + CF_WIDTH:OFF_CF].astype(BF16)
    wgate = w[:, OFF_CF:].astype(BF16)
    pad_heads = lambda a: jnp.pad(a.astype(F32), (0, LANES - SSD_HEADS)).reshape(1, LANES)
    head_expand = (jnp.arange(SSD_D_INNER)[None, :] // SSD_HEADDIM
                   == jnp.arange(LANES)[:, None]).astype(F32)

    zs = _proj_call(h0b, [wz], _silu, BF16, tm_proj, 512, name="proj_z")
    xbc = _proj_call(h0b, [wxbc], lambda a: a, BF16, tm_proj, 512, name="proj_xbc")
    dt = _proj_call(h0b, [wdt], _softplus, F32, tm_proj, LANES, bias=pad_heads(ssd_dt_bias[0]),
                    name="proj_dt")
    cglu = _proj_call(h0b, [wcv, wcg], lambda a, g: a * jax.nn.sigmoid(g), BF16, tm_proj, 512,
                      name="proj_glu")
    gates = _proj_call(h0b, [wgate], jax.nn.sigmoid, BF16, tm_proj, 512, name="proj_gate")

    yn = _ssd_call(xbc, zs, dt, ssd_conv_w[0], ssd_conv_b[0].reshape(1, -1), pad_heads(ssd_a_log[0]),
                   jnp.repeat(ssd_d[0].astype(F32), SSD_HEADDIM).reshape(1, -1),
                   ssd_norm_w[0].reshape(1, -1), head_expand, bsz, lp, rows)
    c2 = _conf_call(cglu, cf_dw_w[0], cf_dw_b[0].reshape(1, -1), cf_ln_g[0].reshape(1, -1),
                    cf_ln_b[0].reshape(1, -1), bsz, lp, rows)
    h1, h1b = _merge_call(yn, c2, gates, h0, ssd_out[0].astype(BF16), cf_out[0].astype(BF16),
                          w_o[0].astype(BF16), ln1_g[0].reshape(1, -1), ln1_b[0].reshape(1, -1),
                          tm_merge)

    keys = peer_keys[0]
    eye_h = jnp.eye(PEER_HEADS, dtype=keys.dtype)
    half_cols = PEER_DKEY // 2
    kbig = []
    for half in range(2):
        blk = jnp.einsum('hnk,hg->nhgk', keys[half], eye_h)
        full = jnp.zeros((PEER_NKEYS, PEER_HEADS, PEER_HEADS, 2, half_cols), keys.dtype)
        full = full.at[:, :, :, half, :].set(blk)
        kbig.append(full.reshape(PEER_NKEYS * PEER_HEADS, PEER_HEADS * PEER_DKEY))
    kbig = jnp.stack(kbig).astype(BF16)

    i1, i2, gate = _peer_query_call(h1b, peer_wq[0].astype(BF16), kbig, tq)
    out = _peer_expert_call(h1b, h1, i1, i2, gate, peer_u[0].astype(BF16), peer_v[0].astype(BF16),
                            ln2_g[0].reshape(1, -1), ln2_b[0].reshape(1, -1), tm_peer, 512)
    return out.reshape(bsz, lp, d)[:, PAD_ROWS + N_META:]
```

```python
import functools
import math

import jax
import jax.numpy as jnp
from jax import lax
from jax.experimental import pallas as pl
from jax.experimental.pallas import tpu as pltpu

F32 = jnp.float32
BF16 = jnp.bfloat16

D_MODEL = 2048
CHUNK = 64
N_META = 16
PAD_ROWS = CHUNK - N_META
SSD_D_INNER = 2048
SSD_HEADDIM = 64
SSD_HEADS = 32
SSD_GROUPS = 8
SSD_STATE = 128
SSD_CONV = 4
SSD_XBC = SSD_D_INNER + 2 * SSD_GROUPS * SSD_STATE
GROUP_COLS = SSD_D_INNER // SSD_GROUPS
HEADS_PER_GROUP = SSD_HEADS // SSD_GROUPS
CF_WIDTH = 2048
CF_CONV = 31
CF_HALO = 32
PEER_HEADS = 8
PEER_NKEYS = 128
PEER_EXPERTS = PEER_NKEYS * PEER_NKEYS
PEER_DKEY = 256
PEER_TOPK = 16
OFF_Z = SSD_D_INNER
OFF_XBC = OFF_Z + SSD_XBC
OFF_DT = OFF_XBC + SSD_HEADS
OFF_CF = OFF_DT + 2 * CF_WIDTH
LN_EPS = 1e-5
DN_ALPHA = 2.0 ** 0.25
LANES = 128
VMEM_LIMIT = 56 * 1024 * 1024

_CAND = [(a, b) for a in range(PEER_TOPK) for b in range(PEER_TOPK)
         if (a + 1) * (b + 1) <= PEER_TOPK]


PEER_VMEM_LIMIT = 60 * 1024 * 1024


def _cparams(sem, vmem_limit=VMEM_LIMIT):
    return pltpu.CompilerParams(dimension_semantics=sem, vmem_limit_bytes=vmem_limit)


def _resident(shape):
    nd = len(shape)
    return pl.BlockSpec(shape, lambda *_: (0,) * nd, pipeline_mode=pl.Buffered(1))


def _divisor_tile(n, target, mult):
    best = None
    for t in range(mult, min(n, target) + 1, mult):
        if n % t == 0:
            best = t
    assert best is not None, (n, target, mult)
    return best


def _layer_norm(x, g, b):
    mu = jnp.mean(x, axis=-1, keepdims=True)
    xc = x - mu
    var = jnp.mean(xc * xc, axis=-1, keepdims=True)
    return xc * lax.rsqrt(var + LN_EPS) * g + b


def _silu(x):
    return x * jax.nn.sigmoid(x)


def _ln_kernel(x_ref, g_ref, b_ref, o_ref, ob_ref):
    y = _layer_norm(x_ref[...], g_ref[...], b_ref[...])
    o_ref[...] = y
    ob_ref[...] = y.astype(BF16)


def _ln_call(x, g, b, rows):
    tp, d = x.shape
    return pl.pallas_call(
        _ln_kernel,
        grid=(tp // rows,),
        in_specs=[pl.BlockSpec((rows, d), lambda i: (i, 0)),
                  pl.BlockSpec((1, d), lambda i: (0, 0)),
                  pl.BlockSpec((1, d), lambda i: (0, 0))],
        out_specs=[pl.BlockSpec((rows, d), lambda i: (i, 0)),
                   pl.BlockSpec((rows, d), lambda i: (i, 0))],
        out_shape=[jax.ShapeDtypeStruct((tp, d), F32), jax.ShapeDtypeStruct((tp, d), BF16)],
        compiler_params=_cparams(("parallel",)),
        name="ln0",
    )(x, g.reshape(1, d), b.reshape(1, d))


def _proj_kernel(x_ref, *refs, n_w, epilogue, has_bias):
    w_refs = refs[:n_w]
    b_ref = refs[n_w] if has_bias else None
    o_ref = refs[-1]
    x = x_ref[...]
    accs = [jnp.dot(x, w[...], preferred_element_type=F32) for w in w_refs]
    if has_bias:
        accs[0] = accs[0] + b_ref[...]
    o_ref[...] = epilogue(*accs).astype(o_ref.dtype)


def _proj_call(x, ws, epilogue, out_dtype, tm, tn, bias=None, name="proj"):
    tp, k = x.shape
    n = ws[0].shape[1]
    in_specs = [pl.BlockSpec((tm, k), lambda i, j: (i, 0))]
    in_specs += [pl.BlockSpec((k, tn), lambda i, j: (0, j)) for _ in ws]
    args = [x, *ws]
    if bias is not None:
        in_specs.append(pl.BlockSpec((1, tn), lambda i, j: (0, j)))
        args.append(bias)
    return pl.pallas_call(
        functools.partial(_proj_kernel, n_w=len(ws), epilogue=epilogue, has_bias=bias is not None),
        grid=(tp // tm, n // tn),
        in_specs=in_specs,
        out_specs=pl.BlockSpec((tm, tn), lambda i, j: (i, j)),
        out_shape=jax.ShapeDtypeStruct((tp, n), out_dtype),
        compiler_params=_cparams(("parallel", "parallel")),
        name=name,
    )(*args)


def _softplus(x):
    return jnp.maximum(x, 0.0) + jnp.log1p(jnp.exp(-jnp.abs(x)))


def _split3(x):
    hi = x.astype(BF16)
    r1 = x - hi.astype(F32)
    mid = r1.astype(BF16)
    lo = (r1 - mid.astype(F32)).astype(BF16)
    return jnp.concatenate([hi, mid, lo], axis=1)


def _ssd_kernel(xbc_ref, zs_ref, dt_ref, cw_ref, cb_ref, alog_ref, dskip_ref, nw_ref, hexp_ref,
                o_ref, ext_ref, state_ref, y_ref, *, rows):
    i = pl.program_id(1)

    @pl.when(i == 0)
    def _():
        ext_ref[0:8, :] = jnp.zeros((8, SSD_XBC), F32)
        state_ref[...] = jnp.zeros(state_ref.shape, F32)

    row = i * rows + lax.broadcasted_iota(jnp.int32, (rows, 1), 0)
    valid = row >= PAD_ROWS
    ext_ref[8:8 + rows, :] = jnp.where(valid, xbc_ref[...].astype(F32), 0.0)
    acc = cb_ref[...] + cw_ref[0:1, :] * ext_ref[pl.ds(8 - (SSD_CONV - 1), rows), :]
    for k in range(1, SSD_CONV):
        acc = acc + cw_ref[k:k + 1, :] * ext_ref[pl.ds(8 - (SSD_CONV - 1) + k, rows), :]
    ext_ref[0:8, :] = ext_ref[rows:rows + 8, :]
    xc = _silu(acc)
    xs = xc[:, :SSD_D_INNER]
    bm = xc[:, SSD_D_INNER:SSD_D_INNER + SSD_GROUPS * SSD_STATE].astype(BF16)
    cm = xc[:, SSD_D_INNER + SSD_GROUPS * SSD_STATE:].astype(BF16)

    dt = jnp.where(valid, dt_ref[...], 0.0)
    da = dt * (-jnp.exp(alog_ref[...]))
    pos = lax.broadcasted_iota(jnp.int32, (rows, 1), 0) % CHUNK
    cs = da
    shift = 1
    while shift < CHUNK:
        cs = cs + jnp.where(pos >= shift, pltpu.roll(cs, shift, axis=0), 0.0)
        shift *= 2
    hexp = hexp_ref[...]
    dt_e = jnp.dot(_split3(dt), hexp, preferred_element_type=F32)
    cs_e = jnp.dot(_split3(cs), hexp, preferred_element_type=F32)
    xdt = xs * dt_e

    sub = lax.broadcasted_iota(jnp.int32, (CHUNK, SSD_D_INNER), 0)
    lane_pos = lax.broadcasted_iota(jnp.int32, (CHUNK, SSD_D_INNER), 1) % SSD_HEADDIM
    diag = sub == lane_pos
    causal = sub >= lane_pos
    blk_r = lax.broadcasted_iota(jnp.int32, (GROUP_COLS, GROUP_COLS), 0) // SSD_HEADDIM
    blk_c = lax.broadcasted_iota(jnp.int32, (GROUP_COLS, GROUP_COLS), 1) // SSD_HEADDIM
    same_head = blk_r == blk_c

    for c in range(rows // CHUNK):
        sl = slice(c * CHUNK, (c + 1) * CHUNK)
        cs_c = cs_e[sl]
        cs_row = jnp.sum(jnp.where(diag, cs_c, 0.0), axis=0, keepdims=True)
        cs_end = cs_c[CHUNK - 1:CHUNK, :]
        decay_l = jnp.where(causal, jnp.exp(cs_c - cs_row), 0.0)
        decay_in = jnp.exp(cs_c)
        decay_out = jnp.exp(cs_end - cs_c)
        decay_chunk = jnp.exp(cs_end)
        for g in range(SSD_GROUPS):
            gc = slice(g * GROUP_COLS, (g + 1) * GROUP_COLS)
            gn = slice(g * SSD_STATE, (g + 1) * SSD_STATE)
            cm_g = cm[sl, gn]
            bm_g = bm[sl, gn]
            bm_rep = jnp.concatenate([bm_g] * HEADS_PER_GROUP, axis=0)
            scores = lax.dot_general(cm_g, bm_rep, (((1,), (1,)), ((), ())),
                                     preferred_element_type=F32)
            m = (scores * decay_l[:, gc]).astype(BF16)
            x_g = xdt[sl, gc]
            x_rep = jnp.concatenate([x_g] * HEADS_PER_GROUP, axis=0)
            x_bd = jnp.where(same_head, x_rep, 0.0).astype(BF16)
            y_diag = jnp.dot(m, x_bd, preferred_element_type=F32)
            st = state_ref[g]
            y_off = jnp.dot(cm_g, st.astype(BF16), preferred_element_type=F32) * decay_in[:, gc]
            y_ref[sl, gc] = y_diag + y_off
            xd = (x_g * decay_out[:, gc]).astype(BF16)
            upd = lax.dot_general(bm_g, xd, (((0,), (0,)), ((), ())), preferred_element_type=F32)
            state_ref[g] = st * decay_chunk[:, gc] + upd

    y = y_ref[...] + dskip_ref[...] * xs
    yf = y * zs_ref[...].astype(F32)
    for g in range(SSD_GROUPS):
        gc = slice(g * GROUP_COLS, (g + 1) * GROUP_COLS)
        seg = yf[:, gc]
        ms = jnp.mean(seg * seg, axis=-1, keepdims=True)
        o_ref[:, gc] = (seg * lax.rsqrt(ms + LN_EPS) * nw_ref[:, gc]).astype(o_ref.dtype)


def _ssd_call(xbc, zs, dt, cw, cb, alog, dskip, nw, hexp, bsz, lp, rows):
    nblk = lp // rows
    rmap = lambda b, i: (b * nblk + i, 0)
    cmap = lambda b, i: (0, 0)
    return pl.pallas_call(
        functools.partial(_ssd_kernel, rows=rows),
        grid=(bsz, nblk),
        in_specs=[pl.BlockSpec((rows, SSD_XBC), rmap),
                  pl.BlockSpec((rows, SSD_D_INNER), rmap),
                  pl.BlockSpec((rows, LANES), rmap),
                  pl.BlockSpec((SSD_CONV, SSD_XBC), cmap),
                  pl.BlockSpec((1, SSD_XBC), cmap),
                  pl.BlockSpec((1, LANES), cmap),
                  pl.BlockSpec((1, SSD_D_INNER), cmap),
                  pl.BlockSpec((1, SSD_D_INNER), cmap),
                  pl.BlockSpec((3 * LANES, SSD_D_INNER), cmap)],
        out_specs=pl.BlockSpec((rows, SSD_D_INNER), rmap),
        out_shape=jax.ShapeDtypeStruct((bsz * lp, SSD_D_INNER), BF16),
        scratch_shapes=[pltpu.VMEM((8 + rows, SSD_XBC), F32),
                        pltpu.VMEM((SSD_GROUPS, SSD_STATE, GROUP_COLS), F32),
                        pltpu.VMEM((rows, SSD_D_INNER), F32)],
        compiler_params=_cparams(("arbitrary", "arbitrary")),
        name="ssd",
    )(xbc, zs, dt, cw, cb, alog, dskip, nw, hexp)


CF_ROW_BLK = 64
CF_COL_BLK = 256
SUBLANES = 8


def _conf_kernel(c_ref, w_ref, b_ref, g_ref, beta_ref, o_ref, ext_ref, sh_ref, acc_ref, *, rows):
    i = pl.program_id(1)

    @pl.when(i == 0)
    def _():
        ext_ref[0:CF_HALO, :] = jnp.zeros((CF_HALO, CF_WIDTH), F32)

    row = i * rows + lax.broadcasted_iota(jnp.int32, (rows, 1), 0)
    ext_ref[CF_HALO:CF_HALO + rows, :] = jnp.where(row >= PAD_ROWS, c_ref[...].astype(F32), 0.0)
    span = rows + CF_HALO - SUBLANES
    for s in range(1, SUBLANES):
        sh_ref[s - 1] = ext_ref[pl.ds(s, span), :]
    base = CF_HALO - (CF_CONV - 1)

    col_blocks = CF_WIDTH // CF_COL_BLK

    def tile(it, carry):
        r0 = pl.multiple_of((it // col_blocks) * CF_ROW_BLK, CF_ROW_BLK)
        cols = pl.ds(pl.multiple_of((it % col_blocks) * CF_COL_BLK, CF_COL_BLK), CF_COL_BLK)
        acc = jnp.broadcast_to(b_ref[:, cols], (CF_ROW_BLK, CF_COL_BLK))
        for k in range(CF_CONV):
            s = (base + k) % SUBLANES
            q = pl.multiple_of(r0 + (base + k - s), SUBLANES)
            if s == 0:
                src = ext_ref[pl.ds(q, CF_ROW_BLK), cols]
            else:
                src = sh_ref[s - 1, pl.ds(q, CF_ROW_BLK), cols]
            acc = acc + w_ref[k:k + 1, cols] * src
        acc_ref[pl.ds(r0, CF_ROW_BLK), cols] = acc
        return carry

    lax.fori_loop(0, (rows // CF_ROW_BLK) * col_blocks, tile, 0)
    ext_ref[0:CF_HALO, :] = ext_ref[rows:rows + CF_HALO, :]
    o_ref[...] = _silu(_layer_norm(acc_ref[...], g_ref[...], beta_ref[...])).astype(o_ref.dtype)


def _conf_call(c, w, b, g, beta, bsz, lp, rows):
    nblk = lp // rows
    rmap = lambda bb, i: (bb * nblk + i, 0)
    cmap = lambda bb, i: (0, 0)
    return pl.pallas_call(
        functools.partial(_conf_kernel, rows=rows),
        grid=(bsz, nblk),
        in_specs=[pl.BlockSpec((rows, CF_WIDTH), rmap),
                  pl.BlockSpec((CF_CONV, CF_WIDTH), cmap),
                  pl.BlockSpec((1, CF_WIDTH), cmap),
                  pl.BlockSpec((1, CF_WIDTH), cmap),
                  pl.BlockSpec((1, CF_WIDTH), cmap)],
        out_specs=pl.BlockSpec((rows, CF_WIDTH), rmap),
        out_shape=jax.ShapeDtypeStruct((bsz * lp, CF_WIDTH), BF16),
        scratch_shapes=[pltpu.VMEM((CF_HALO + rows, CF_WIDTH), F32),
                        pltpu.VMEM((SUBLANES - 1, CF_HALO + rows - SUBLANES, CF_WIDTH), F32),
                        pltpu.VMEM((rows, CF_WIDTH), F32)],
        compiler_params=_cparams(("arbitrary", "arbitrary")),
        name="conformer",
    )(c, w, b, g, beta)


def _merge_kernel(yn_ref, c_ref, gate_ref, h_ref, wssd_ref, wcf_ref, wo_ref, g_ref, b_ref,
                  o_ref, ob_ref):
    y_ssd = jnp.dot(yn_ref[...], wssd_ref[...], preferred_element_type=F32)
    y_cf = jnp.dot(c_ref[...], wcf_ref[...], preferred_element_type=F32)
    gates = gate_ref[...].astype(F32)
    mix = gates[:, :D_MODEL] * y_ssd + gates[:, D_MODEL:] * y_cf
    m = jnp.dot(mix.astype(BF16), wo_ref[...], preferred_element_type=F32)
    h1 = _layer_norm(DN_ALPHA * h_ref[...] + m, g_ref[...], b_ref[...])
    o_ref[...] = h1
    ob_ref[...] = h1.astype(BF16)


def _merge_call(yn, c, gates, h0, wssd, wcf, wo, g, b, tm):
    tp = yn.shape[0]
    rmap = lambda i: (i, 0)
    return pl.pallas_call(
        _merge_kernel,
        grid=(tp // tm,),
        in_specs=[pl.BlockSpec((tm, SSD_D_INNER), rmap),
                  pl.BlockSpec((tm, CF_WIDTH), rmap),
                  pl.BlockSpec((tm, 2 * D_MODEL), rmap),
                  pl.BlockSpec((tm, D_MODEL), rmap),
                  _resident((SSD_D_INNER, D_MODEL)),
                  _resident((CF_WIDTH, D_MODEL)),
                  _resident((D_MODEL, D_MODEL)),
                  _resident((1, D_MODEL)),
                  _resident((1, D_MODEL))],
        out_specs=[pl.BlockSpec((tm, D_MODEL), rmap), pl.BlockSpec((tm, D_MODEL), rmap)],
        out_shape=[jax.ShapeDtypeStruct((tp, D_MODEL), F32),
                   jax.ShapeDtypeStruct((tp, D_MODEL), BF16)],
        compiler_params=_cparams(("parallel",)),
        name="merge",
    )(yn, c, gates, h0, wssd, wcf, wo, g, b)


def _top16_major(s_ref, val_ref, idx_ref, n):
    neg = jnp.float32(-jnp.inf)
    m0 = s_ref[0]
    for j in range(1, n):
        m0 = jnp.maximum(m0, s_ref[j])

    def body(it, m):
        idx = jnp.full(m.shape, n, jnp.int32)
        for j in range(n):
            idx = jnp.minimum(idx, jnp.where(s_ref[j] == m, j, n))
        val_ref[it] = m
        idx_ref[it] = idx
        nxt = jnp.full(m.shape, neg, F32)
        for j in range(n):
            sj = jnp.where(idx == j, neg, s_ref[j])
            s_ref[j] = sj
            nxt = jnp.maximum(nxt, sj)
        return nxt

    lax.fori_loop(0, PEER_TOPK, body, m0)


def _peer_query_kernel(hb_ref, wq_ref, kbig_ref, i1_ref, i2_ref, gate_ref,
                       s_ref, v1_ref, x1_ref, v2_ref, x2_ref, cand_ref, bs_ref, e1_ref, e2_ref):
    tq = hb_ref.shape[0]
    q = jnp.dot(hb_ref[...], wq_ref[...], preferred_element_type=F32).astype(BF16)
    for half, (v_ref, x_ref) in enumerate(((v1_ref, x1_ref), (v2_ref, x2_ref))):
        st = lax.dot_general(kbig_ref[half], q, (((1,), (1,)), ((), ())),
                             preferred_element_type=F32)
        s_ref[...] = st.reshape(PEER_NKEYS, PEER_HEADS, tq)
        _top16_major(s_ref, v_ref, x_ref, PEER_NKEYS)

    ncand = len(_CAND)
    for ci, (a, b) in enumerate(_CAND):
        cand_ref[ci] = v1_ref[a] + v2_ref[b]
    neg = jnp.float32(-jnp.inf)
    m0 = cand_ref[0]
    for ci in range(1, ncand):
        m0 = jnp.maximum(m0, cand_ref[ci])
    big = PEER_TOPK * PEER_TOPK

    def body(it, m):
        sel = jnp.full(m.shape, big, jnp.int32)
        for ci, (a, b) in enumerate(_CAND):
            sel = jnp.minimum(sel, jnp.where(cand_ref[ci] == m, a * PEER_TOPK + b, big))
        e1 = jnp.zeros(m.shape, jnp.int32)
        e2 = jnp.zeros(m.shape, jnp.int32)
        nxt = jnp.full(m.shape, neg, F32)
        for ci, (a, b) in enumerate(_CAND):
            hit = sel == a * PEER_TOPK + b
            e1 = jnp.where(hit, x1_ref[a], e1)
            e2 = jnp.where(hit, x2_ref[b], e2)
            cj = jnp.where(hit, neg, cand_ref[ci])
            cand_ref[ci] = cj
            nxt = jnp.maximum(nxt, cj)
        bs_ref[it] = m
        e1_ref[it] = e1
        e2_ref[it] = e2
        return nxt

    lax.fori_loop(0, PEER_TOPK, body, m0)

    bs = bs_ref[...]
    ex = jnp.exp(bs - bs[0:1])
    gate = ex / jnp.sum(ex, axis=0, keepdims=True)
    nj = PEER_TOPK * PEER_HEADS
    gate_ref[...] = gate.reshape(nj, tq).T
    i1_ref[...] = e1_ref[...].reshape(nj, tq).T
    i2_ref[...] = e2_ref[...].reshape(nj, tq).T


def _peer_query_call(hb, wq, kbig, tq):
    tp = hb.shape[0]
    nj = PEER_TOPK * PEER_HEADS
    rmap = lambda i: (i, 0)
    slab = lambda n, dt: pltpu.VMEM((n, PEER_HEADS, tq), dt)
    return pl.pallas_call(
        _peer_query_kernel,
        grid=(tp // tq,),
        in_specs=[pl.BlockSpec((tq, D_MODEL), rmap),
                  _resident((D_MODEL, PEER_HEADS * PEER_DKEY)),
                  _resident((2, PEER_NKEYS * PEER_HEADS, PEER_HEADS * PEER_DKEY))],
        out_specs=[pl.BlockSpec((tq, nj), rmap)] * 3,
        out_shape=[jax.ShapeDtypeStruct((tp, nj), jnp.int32),
                   jax.ShapeDtypeStruct((tp, nj), jnp.int32),
                   jax.ShapeDtypeStruct((tp, nj), F32)],
        scratch_shapes=[slab(PEER_NKEYS, F32),
                        slab(PEER_TOPK, F32), slab(PEER_TOPK, jnp.int32),
                        slab(PEER_TOPK, F32), slab(PEER_TOPK, jnp.int32),
                        slab(len(_CAND), F32), slab(PEER_TOPK, F32),
                        slab(PEER_TOPK, jnp.int32), slab(PEER_TOPK, jnp.int32)],
        compiler_params=_cparams(("parallel",)),
        name="peer_query",
    )(hb, wq, kbig)


def _gelu(x):
    return 0.5 * x * (1.0 + lax.erf(x * (1.0 / math.sqrt(2.0))))


BUILD_UNROLL = 8


def _pack_bf16_pair(a, b):
    bits = lambda v: lax.bitcast_convert_type(v.astype(BF16).astype(F32), jnp.uint32)
    return bits(a) | (bits(b) >> 16)


def _unpack_bf16_pair(w):
    return (lax.bitcast_convert_type(w & jnp.uint32(0xFFFF0000), F32),
            lax.bitcast_convert_type(w << 16, F32))


def _route_rows(i1_row, i2_row, gate_row, key_iota):
    a_t = jnp.where(key_iota == i1_row, gate_row, 0.0).astype(BF16)
    b_t = jnp.where(key_iota == i2_row, 1.0, 0.0).astype(BF16)
    return lax.dot_general(a_t, b_t, (((1,), (1,)), ((), ())), preferred_element_type=F32)


def _peer_expert_kernel(hb_ref, h_ref, i1_ref, i2_ref, gate_ref, u_ref, v_ref, g_ref, b_ref,
                        o_ref, gs_ref, *, tm, te, stride):
    e = pl.program_id(1)
    half = tm // 2

    @pl.when(e == 0)
    def _():
        key_iota = lax.broadcasted_iota(jnp.int32, (PEER_NKEYS, PEER_NKEYS), 0)

        def build(blk, carry):
            lo = pl.multiple_of(blk * BUILD_UNROLL, BUILD_UNROLL)
            hi = pl.multiple_of(half + blk * BUILD_UNROLL, BUILD_UNROLL)
            rows = [(r[pl.ds(lo, BUILD_UNROLL), :], r[pl.ds(hi, BUILD_UNROLL), :])
                    for r in (i1_ref, i2_ref, gate_ref)]
            for k in range(BUILD_UNROLL):
                g0 = _route_rows(rows[0][0][k:k + 1], rows[1][0][k:k + 1], rows[2][0][k:k + 1], key_iota)
                g1 = _route_rows(rows[0][1][k:k + 1], rows[1][1][k:k + 1], rows[2][1][k:k + 1], key_iota)
                gs_ref[pl.ds(lo + k, PEER_NKEYS, stride=stride), :] = _pack_bf16_pair(g0, g1)
            return carry

        lax.fori_loop(0, half // BUILD_UNROLL, build, 0)

    s = lax.dot_general(hb_ref[...], u_ref[...], (((1,), (1,)), ((), ())),
                        preferred_element_type=F32)
    slabs = te // PEER_NKEYS
    packed = jnp.concatenate(
        [gs_ref[pl.ds(pl.multiple_of((e * slabs + r) * stride, 8), half), :] for r in range(slabs)],
        axis=1)
    route = jnp.concatenate(_unpack_bf16_pair(packed), axis=0)
    act = (_gelu(s) * route).astype(BF16)
    f = jnp.dot(act, v_ref[...], preferred_element_type=F32)

    @pl.when(e == 0)
    def _():
        o_ref[...] = f

    @pl.when(e > 0)
    def _():
        o_ref[...] += f

    @pl.when(e == pl.num_programs(1) - 1)
    def _():
        o_ref[...] = _layer_norm(DN_ALPHA * h_ref[...] + o_ref[...], g_ref[...], b_ref[...])


def _peer_expert_call(hb, h, i1, i2, gate, u, v, g, b, tm, te):
    tp = hb.shape[0]
    nj = PEER_TOPK * PEER_HEADS
    assert tm % (2 * BUILD_UNROLL) == 0
    stride = tm // 2 + 8
    rmap = lambda i, e: (i, 0)
    emap = lambda i, e: (e, 0)
    cmap = lambda i, e: (0, 0)
    once = dict(pipeline_mode=pl.Buffered(1))
    return pl.pallas_call(
        functools.partial(_peer_expert_kernel, tm=tm, te=te, stride=stride),
        grid=(tp // tm, PEER_EXPERTS // te),
        in_specs=[pl.BlockSpec((tm, D_MODEL), rmap, **once),
                  pl.BlockSpec((tm, D_MODEL), rmap, **once),
                  pl.BlockSpec((tm, nj), rmap, **once),
                  pl.BlockSpec((tm, nj), rmap, **once),
                  pl.BlockSpec((tm, nj), rmap, **once),
                  pl.BlockSpec((te, D_MODEL), emap),
                  pl.BlockSpec((te, D_MODEL), emap),
                  pl.BlockSpec((1, D_MODEL), cmap),
                  pl.BlockSpec((1, D_MODEL), cmap)],
        out_specs=pl.BlockSpec((tm, D_MODEL), rmap),
        out_shape=jax.ShapeDtypeStruct((tp, D_MODEL), F32),
        scratch_shapes=[pltpu.VMEM((PEER_NKEYS * stride, PEER_NKEYS), jnp.uint32)],
        compiler_params=_cparams(("parallel", "arbitrary"), PEER_VMEM_LIMIT),
        name="peer_experts",
    )(hb, h, i1, i2, gate, u, v, g, b)


def kernel(x, meta, ln0_g, ln0_b, w_in, ssd_conv_w, ssd_conv_b, ssd_dt_bias, ssd_a_log, ssd_d,
           ssd_norm_w, ssd_out, cf_dw_w, cf_dw_b, cf_ln_g, cf_ln_b, cf_out, w_o, ln1_g, ln1_b,
           peer_wq, peer_keys, peer_u, peer_v, ln2_g, ln2_b):
    bsz, seq, d = x.shape
    assert d == D_MODEL and seq % CHUNK == 0
    assert w_in.shape[0] == 1, "single layer"
    lp = PAD_ROWS + N_META + seq
    tp = bsz * lp
    assert tp % LANES == 0

    rows = _divisor_tile(lp, 320, CHUNK)
    tm_proj = _divisor_tile(tp, 1664, 8)
    tm_merge = _divisor_tile(tp, 320, 8)
    tq = _divisor_tile(tp, 640, LANES)
    tm_peer = _divisor_tile(tp, 416, 2 * BUILD_UNROLL)

    hcat = jnp.concatenate(
        [jnp.zeros((bsz, PAD_ROWS, d), x.dtype),
         jnp.broadcast_to(meta[None].astype(x.dtype), (bsz, N_META, d)), x], axis=1).reshape(tp, d)
    h0, h0b = _ln_call(hcat, ln0_g, ln0_b, _divisor_tile(tp, 320, 8))

    w = w_in[0]
    wz = w[:, :OFF_Z].astype(BF16)
    wxbc = w[:, OFF_Z:OFF_XBC].astype(BF16)
    wdt = jnp.pad(w[:, OFF_XBC:OFF_DT], ((0, 0), (0, LANES - SSD_HEADS))).astype(BF16)
    wcv = w[:, OFF_DT:OFF_DT + CF_WIDTH].astype(BF16)
    wcg = w[:, OFF_DT + CF_WIDTH:OFF_CF].astype(BF16)
    wgate = w[:, OFF_CF:].astype(BF16)
    pad_heads = lambda a: jnp.pad(a.astype(F32), (0, LANES - SSD_HEADS)).reshape(1, LANES)
    head_expand = jnp.tile(jnp.arange(SSD_D_INNER)[None, :] // SSD_HEADDIM
                           == jnp.arange(LANES)[:, None], (3, 1)).astype(BF16)

    zs = _proj_call(h0b, [wz], _silu, BF16, tm_proj, 512, name="proj_z")
    xbc = _proj_call(h0b, [wxbc], lambda a: a, BF16, tm_proj, 512, name="proj_xbc")
    dt = _proj_call(h0b, [wdt], _softplus, F32, tm_proj, LANES, bias=pad_heads(ssd_dt_bias[0]),
                    name="proj_dt")
    cglu = _proj_call(h0b, [wcv, wcg], lambda a, g: a * jax.nn.sigmoid(g), BF16, tm_proj, 512,
                      name="proj_glu")
    gates = _proj_call(h0b, [wgate], jax.nn.sigmoid, BF16, tm_proj, 512, name="proj_gate")

    yn = _ssd_call(xbc, zs, dt, ssd_conv_w[0], ssd_conv_b[0].reshape(1, -1), pad_heads(ssd_a_log[0]),
                   jnp.repeat(ssd_d[0].astype(F32), SSD_HEADDIM).reshape(1, -1),
                   ssd_norm_w[0].reshape(1, -1), head_expand, bsz, lp, rows)
    c2 = _conf_call(cglu, cf_dw_w[0], cf_dw_b[0].reshape(1, -1), cf_ln_g[0].reshape(1, -1),
                    cf_ln_b[0].reshape(1, -1), bsz, lp, rows)
    h1, h1b = _merge_call(yn, c2, gates, h0, ssd_out[0].astype(BF16), cf_out[0].astype(BF16),
                          w_o[0].astype(BF16), ln1_g[0].reshape(1, -1), ln1_b[0].reshape(1, -1),
                          tm_merge)

    keys = peer_keys[0]
    eye_h = jnp.eye(PEER_HEADS, dtype=keys.dtype)
    half_cols = PEER_DKEY // 2
    kbig = []
    for half in range(2):
        blk = jnp.einsum('hnk,hg->nhgk', keys[half], eye_h)
        full = jnp.zeros((PEER_NKEYS, PEER_HEADS, PEER_HEADS, 2, half_cols), keys.dtype)
        full = full.at[:, :, :, half, :].set(blk)
        kbig.append(full.reshape(PEER_NKEYS * PEER_HEADS, PEER_HEADS * PEER_DKEY))
    kbig = jnp.stack(kbig).astype(BF16)

    i1, i2, gate = _peer_query_call(h1b, peer_wq[0].astype(BF16), kbig, tq)
    out = _peer_expert_call(h1b, h1, i1, i2, gate, peer_u[0].astype(BF16), peer_v[0].astype(BF16),
                            ln2_g[0].reshape(1, -1), ln2_b[0].reshape(1, -1), tm_peer, 1024)
    return out.reshape(bsz, lp, d)[:, PAD_ROWS + N_META:]
```

```python
import functools
import math

import jax
import jax.numpy as jnp
from jax import lax
from jax.experimental import pallas as pl
from jax.experimental.pallas import tpu as pltpu

F32 = jnp.float32
BF16 = jnp.bfloat16

D_MODEL = 2048
CHUNK = 64
N_META = 16
PAD_ROWS = CHUNK - N_META
SSD_D_INNER = 2048
SSD_HEADDIM = 64
SSD_HEADS = 32
SSD_GROUPS = 8
SSD_STATE = 128
SSD_CONV = 4
SSD_XBC = SSD_D_INNER + 2 * SSD_GROUPS * SSD_STATE
GROUP_COLS = SSD_D_INNER // SSD_GROUPS
HEADS_PER_GROUP = SSD_HEADS // SSD_GROUPS
CF_WIDTH = 2048
CF_CONV = 31
CF_HALO = 32
PEER_HEADS = 8
PEER_NKEYS = 128
PEER_EXPERTS = PEER_NKEYS * PEER_NKEYS
PEER_DKEY = 256
PEER_TOPK = 16
OFF_Z = SSD_D_INNER
OFF_XBC = OFF_Z + SSD_XBC
OFF_DT = OFF_XBC + SSD_HEADS
OFF_CF = OFF_DT + 2 * CF_WIDTH
LN_EPS = 1e-5
DN_ALPHA = 2.0 ** 0.25
LANES = 128
SUBLANES = 8
VMEM_LIMIT = 56 * 1024 * 1024
PEER_VMEM_LIMIT = 60 * 1024 * 1024

_CAND = [(a, b) for a in range(PEER_TOPK) for b in range(PEER_TOPK)
         if (a + 1) * (b + 1) <= PEER_TOPK]


def _cparams(sem, vmem_limit=VMEM_LIMIT):
    return pltpu.CompilerParams(dimension_semantics=sem, vmem_limit_bytes=vmem_limit)


def _resident(shape):
    nd = len(shape)
    return pl.BlockSpec(shape, lambda *_: (0,) * nd, pipeline_mode=pl.Buffered(1))


def _divisor_tile(n, target, mult):
    best = None
    for t in range(mult, min(n, target) + 1, mult):
        if n % t == 0:
            best = t
    assert best is not None, (n, target, mult)
    return best


def _layer_norm(x, g, b):
    mu = jnp.mean(x, axis=-1, keepdims=True)
    xc = x - mu
    var = jnp.mean(xc * xc, axis=-1, keepdims=True)
    return xc * lax.rsqrt(var + LN_EPS) * g + b


def _silu(x):
    return x * jax.nn.sigmoid(x)


def _ln0_kernel(x_ref, meta_ref, g_ref, b_ref, o_ref, ob_ref, *, n_token_tiles):
    i = pl.program_id(0)

    def emit(rows):
        y = _layer_norm(rows, g_ref[...], b_ref[...])
        o_ref[...] = y
        ob_ref[...] = y.astype(BF16)

    @pl.when(i < n_token_tiles)
    def _():
        emit(x_ref[...])

    @pl.when(i >= n_token_tiles)
    def _():
        head = jnp.concatenate([jnp.zeros((PAD_ROWS, D_MODEL), F32), meta_ref[...]], axis=0)
        emit(jnp.concatenate([head] * (x_ref.shape[0] // CHUNK), axis=0))


def _ln0_call(x2d, meta, g, b, tp, rows):
    t, d = x2d.shape
    n_tok = t // rows
    return pl.pallas_call(
        functools.partial(_ln0_kernel, n_token_tiles=n_tok),
        grid=(tp // rows,),
        in_specs=[pl.BlockSpec((rows, d), lambda i: (jnp.minimum(i, n_tok - 1), 0)),
                  pl.BlockSpec((N_META, d), lambda i: (0, 0)),
                  pl.BlockSpec((1, d), lambda i: (0, 0)),
                  pl.BlockSpec((1, d), lambda i: (0, 0))],
        out_specs=[pl.BlockSpec((rows, d), lambda i: (i, 0)),
                   pl.BlockSpec((rows, d), lambda i: (i, 0))],
        out_shape=[jax.ShapeDtypeStruct((tp, d), F32), jax.ShapeDtypeStruct((tp, d), BF16)],
        compiler_params=_cparams(("parallel",)),
        name="ln0",
    )(x2d, meta, g.reshape(1, d), b.reshape(1, d))


def _proj_kernel(x_ref, *refs, n_w, epilogue, has_bias):
    w_refs = refs[:n_w]
    b_ref = refs[n_w] if has_bias else None
    o_ref = refs[-1]
    x = x_ref[...]
    accs = [jnp.dot(x, w[...], preferred_element_type=F32) for w in w_refs]
    if has_bias:
        accs[0] = accs[0] + b_ref[...]
    o_ref[...] = epilogue(*accs).astype(o_ref.dtype)


def _proj_call(x, ws, epilogue, out_dtype, tm, tn, bias=None, name="proj"):
    tp, k = x.shape
    n = ws[0].shape[1]
    in_specs = [pl.BlockSpec((tm, k), lambda i, j: (i, 0))]
    in_specs += [pl.BlockSpec((k, tn), lambda i, j: (0, j)) for _ in ws]
    args = [x, *ws]
    if bias is not None:
        in_specs.append(pl.BlockSpec((1, tn), lambda i, j: (0, j)))
        args.append(bias)
    return pl.pallas_call(
        functools.partial(_proj_kernel, n_w=len(ws), epilogue=epilogue, has_bias=bias is not None),
        grid=(tp // tm, n // tn),
        in_specs=in_specs,
        out_specs=pl.BlockSpec((tm, tn), lambda i, j: (i, j)),
        out_shape=jax.ShapeDtypeStruct((tp, n), out_dtype),
        compiler_params=_cparams(("parallel", "parallel")),
        name=name,
    )(*args)


def _softplus(x):
    return jnp.maximum(x, 0.0) + jnp.log1p(jnp.exp(-jnp.abs(x)))


def _split3(x):
    hi = x.astype(BF16)
    r1 = x - hi.astype(F32)
    mid = r1.astype(BF16)
    lo = (r1 - mid.astype(F32)).astype(BF16)
    return jnp.concatenate([hi, mid, lo], axis=1)


def _ssd_block(xin, dt, n, refs, y_ref):
    cw_ref, cb_ref, alog_ref, hexp_ref, ext_ref, state_ref = refs
    hist = SUBLANES
    ext_ref[hist:hist + n, :] = xin
    acc = cb_ref[...] + cw_ref[0:1, :] * ext_ref[pl.ds(hist - (SSD_CONV - 1), n), :]
    for k in range(1, SSD_CONV):
        acc = acc + cw_ref[k:k + 1, :] * ext_ref[pl.ds(hist - (SSD_CONV - 1) + k, n), :]
    ext_ref[0:hist, :] = ext_ref[n:n + hist, :]
    xc = _silu(acc)
    xs = xc[:, :SSD_D_INNER]
    bm = xc[:, SSD_D_INNER:SSD_D_INNER + SSD_GROUPS * SSD_STATE].astype(BF16)
    cm = xc[:, SSD_D_INNER + SSD_GROUPS * SSD_STATE:].astype(BF16)

    da = dt * (-jnp.exp(alog_ref[...]))
    pos = lax.broadcasted_iota(jnp.int32, (n, 1), 0) % CHUNK
    cs = da
    shift = 1
    while shift < CHUNK:
        cs = cs + jnp.where(pos >= shift, pltpu.roll(cs, shift, axis=0), 0.0)
        shift *= 2
    hexp = hexp_ref[...]
    dt_e = jnp.dot(_split3(dt), hexp, preferred_element_type=F32)
    cs_e = jnp.dot(_split3(cs), hexp, preferred_element_type=F32)
    xdt = xs * dt_e

    sub = lax.broadcasted_iota(jnp.int32, (CHUNK, SSD_D_INNER), 0)
    lane_pos = lax.broadcasted_iota(jnp.int32, (CHUNK, SSD_D_INNER), 1) % SSD_HEADDIM
    diag = sub == lane_pos
    causal = sub >= lane_pos
    blk_r = lax.broadcasted_iota(jnp.int32, (GROUP_COLS, GROUP_COLS), 0) // SSD_HEADDIM
    blk_c = lax.broadcasted_iota(jnp.int32, (GROUP_COLS, GROUP_COLS), 1) // SSD_HEADDIM
    same_head = blk_r == blk_c

    for c in range(n // CHUNK):
        sl = slice(c * CHUNK, (c + 1) * CHUNK)
        cs_c = cs_e[sl]
        cs_end = cs_c[CHUNK - 1:CHUNK, :]
        decay_out = jnp.exp(cs_end - cs_c)
        decay_chunk = jnp.exp(cs_end)
        if y_ref is not None:
            cs_row = jnp.sum(jnp.where(diag, cs_c, 0.0), axis=0, keepdims=True)
            decay_l = jnp.where(causal, jnp.exp(cs_c - cs_row), 0.0)
            decay_in = jnp.exp(cs_c)
        for g in range(SSD_GROUPS):
            gc = slice(g * GROUP_COLS, (g + 1) * GROUP_COLS)
            gn = slice(g * SSD_STATE, (g + 1) * SSD_STATE)
            bm_g = bm[sl, gn]
            x_g = xdt[sl, gc]
            st = state_ref[g]
            if y_ref is not None:
                cm_g = cm[sl, gn]
                bm_rep = jnp.concatenate([bm_g] * HEADS_PER_GROUP, axis=0)
                scores = lax.dot_general(cm_g, bm_rep, (((1,), (1,)), ((), ())),
                                         preferred_element_type=F32)
                m = (scores * decay_l[:, gc]).astype(BF16)
                x_rep = jnp.concatenate([x_g] * HEADS_PER_GROUP, axis=0)
                x_bd = jnp.where(same_head, x_rep, 0.0).astype(BF16)
                y_diag = jnp.dot(m, x_bd, preferred_element_type=F32)
                y_off = jnp.dot(cm_g, st.astype(BF16), preferred_element_type=F32) * decay_in[:, gc]
                y_ref[sl, gc] = y_diag + y_off
            xd = (x_g * decay_out[:, gc]).astype(BF16)
            upd = lax.dot_general(bm_g, xd, (((0,), (0,)), ((), ())), preferred_element_type=F32)
            state_ref[g] = st * decay_chunk[:, gc] + upd
    return xs


def _ssd_kernel(xbc_ref, zs_ref, dt_ref, xbc_head_ref, dt_head_ref, cw_ref, cb_ref, alog_ref,
                dskip_ref, nw_ref, hexp_ref, o_ref, ext_ref, state_ref, y_ref, *, rows):
    refs = (cw_ref, cb_ref, alog_ref, hexp_ref, ext_ref, state_ref)

    @pl.when(pl.program_id(1) == 0)
    def _():
        ext_ref[0:SUBLANES, :] = jnp.zeros((SUBLANES, SSD_XBC), F32)
        state_ref[...] = jnp.zeros(state_ref.shape, F32)
        meta_row = lax.broadcasted_iota(jnp.int32, (CHUNK, 1), 0) >= PAD_ROWS
        _ssd_block(jnp.where(meta_row, xbc_head_ref[...].astype(F32), 0.0),
                   jnp.where(meta_row, dt_head_ref[...], 0.0), CHUNK, refs, None)

    xs = _ssd_block(xbc_ref[...].astype(F32), dt_ref[...], rows, refs, y_ref)
    y = y_ref[...] + dskip_ref[...] * xs
    yf = y * zs_ref[...].astype(F32)
    for g in range(SSD_GROUPS):
        gc = slice(g * GROUP_COLS, (g + 1) * GROUP_COLS)
        seg = yf[:, gc]
        ms = jnp.mean(seg * seg, axis=-1, keepdims=True)
        o_ref[:, gc] = (seg * lax.rsqrt(ms + LN_EPS) * nw_ref[:, gc]).astype(o_ref.dtype)


def _ssd_call(xbc, zs, dt, cw, cb, alog, dskip, nw, hexp, bsz, seq, rows):
    nblk = seq // rows
    head0 = bsz * seq // CHUNK
    rmap = lambda b, i: (b * nblk + i, 0)
    hmap = lambda b, i: (head0 + b, 0)
    cmap = lambda b, i: (0, 0)
    return pl.pallas_call(
        functools.partial(_ssd_kernel, rows=rows),
        grid=(bsz, nblk),
        in_specs=[pl.BlockSpec((rows, SSD_XBC), rmap),
                  pl.BlockSpec((rows, SSD_D_INNER), rmap),
                  pl.BlockSpec((rows, LANES), rmap),
                  pl.BlockSpec((CHUNK, SSD_XBC), hmap),
                  pl.BlockSpec((CHUNK, LANES), hmap),
                  pl.BlockSpec((SSD_CONV, SSD_XBC), cmap),
                  pl.BlockSpec((1, SSD_XBC), cmap),
                  pl.BlockSpec((1, LANES), cmap),
                  pl.BlockSpec((1, SSD_D_INNER), cmap),
                  pl.BlockSpec((1, SSD_D_INNER), cmap),
                  pl.BlockSpec((3 * LANES, SSD_D_INNER), cmap)],
        out_specs=pl.BlockSpec((rows, SSD_D_INNER), rmap),
        out_shape=jax.ShapeDtypeStruct((bsz * seq, SSD_D_INNER), BF16),
        scratch_shapes=[pltpu.VMEM((SUBLANES + rows, SSD_XBC), F32),
                        pltpu.VMEM((SSD_GROUPS, SSD_STATE, GROUP_COLS), F32),
                        pltpu.VMEM((rows, SSD_D_INNER), F32)],
        compiler_params=_cparams(("arbitrary", "arbitrary")),
        name="ssd",
    )(xbc, zs, dt, xbc, dt, cw, cb, alog, dskip, nw, hexp)


CF_ROW_BLK = 64
CF_COL_BLK = 256


def _conf_kernel(c_ref, head_ref, w_ref, b_ref, g_ref, beta_ref, o_ref, ext_ref, sh_ref, acc_ref,
                 *, rows):
    @pl.when(pl.program_id(1) == 0)
    def _():
        tail = head_ref[CHUNK - CF_HALO:CHUNK, :].astype(F32)
        meta_row = lax.broadcasted_iota(jnp.int32, (CF_HALO, 1), 0) >= CF_HALO - N_META
        ext_ref[0:CF_HALO, :] = jnp.where(meta_row, tail, 0.0)

    ext_ref[CF_HALO:CF_HALO + rows, :] = c_ref[...].astype(F32)
    span = rows + CF_HALO - SUBLANES
    for s in range(1, SUBLANES):
        sh_ref[s - 1] = ext_ref[pl.ds(s, span), :]
    base = CF_HALO - (CF_CONV - 1)
    col_blocks = CF_WIDTH // CF_COL_BLK

    def tile(it, carry):
        r0 = pl.multiple_of((it // col_blocks) * CF_ROW_BLK, CF_ROW_BLK)
        cols = pl.ds(pl.multiple_of((it % col_blocks) * CF_COL_BLK, CF_COL_BLK), CF_COL_BLK)
        acc = jnp.broadcast_to(b_ref[:, cols], (CF_ROW_BLK, CF_COL_BLK))
        for k in range(CF_CONV):
            s = (base + k) % SUBLANES
            q = pl.multiple_of(r0 + (base + k - s), SUBLANES)
            if s == 0:
                src = ext_ref[pl.ds(q, CF_ROW_BLK), cols]
            else:
                src = sh_ref[s - 1, pl.ds(q, CF_ROW_BLK), cols]
            acc = acc + w_ref[k:k + 1, cols] * src
        acc_ref[pl.ds(r0, CF_ROW_BLK), cols] = acc
        return carry

    lax.fori_loop(0, (rows // CF_ROW_BLK) * col_blocks, tile, 0)
    ext_ref[0:CF_HALO, :] = ext_ref[rows:rows + CF_HALO, :]
    o_ref[...] = _silu(_layer_norm(acc_ref[...], g_ref[...], beta_ref[...])).astype(o_ref.dtype)


def _conf_call(c, w, b, g, beta, bsz, seq, rows):
    nblk = seq // rows
    head0 = bsz * seq // CHUNK
    rmap = lambda bb, i: (bb * nblk + i, 0)
    hmap = lambda bb, i: (head0 + bb, 0)
    cmap = lambda bb, i: (0, 0)
    return pl.pallas_call(
        functools.partial(_conf_kernel, rows=rows),
        grid=(bsz, nblk),
        in_specs=[pl.BlockSpec((rows, CF_WIDTH), rmap),
                  pl.BlockSpec((CHUNK, CF_WIDTH), hmap),
                  pl.BlockSpec((CF_CONV, CF_WIDTH), cmap),
                  pl.BlockSpec((1, CF_WIDTH), cmap),
                  pl.BlockSpec((1, CF_WIDTH), cmap),
                  pl.BlockSpec((1, CF_WIDTH), cmap)],
        out_specs=pl.BlockSpec((rows, CF_WIDTH), rmap),
        out_shape=jax.ShapeDtypeStruct((bsz * seq, CF_WIDTH), BF16),
        scratch_shapes=[pltpu.VMEM((CF_HALO + rows, CF_WIDTH), F32),
                        pltpu.VMEM((SUBLANES - 1, CF_HALO + rows - SUBLANES, CF_WIDTH), F32),
                        pltpu.VMEM((rows, CF_WIDTH), F32)],
        compiler_params=_cparams(("arbitrary", "arbitrary")),
        name="conformer",
    )(c, c, w, b, g, beta)


def _merge_kernel(yn_ref, c_ref, gate_ref, h_ref, wssd_ref, wcf_ref, wo_ref, g_ref, b_ref,
                  o_ref, ob_ref):
    y_ssd = jnp.dot(yn_ref[...], wssd_ref[...], preferred_element_type=F32)
    y_cf = jnp.dot(c_ref[...], wcf_ref[...], preferred_element_type=F32)
    gates = gate_ref[...].astype(F32)
    mix = gates[:, :D_MODEL] * y_ssd + gates[:, D_MODEL:] * y_cf
    m = jnp.dot(mix.astype(BF16), wo_ref[...], preferred_element_type=F32)
    h1 = _layer_norm(DN_ALPHA * h_ref[...] + m, g_ref[...], b_ref[...])
    o_ref[...] = h1
    ob_ref[...] = h1.astype(BF16)


def _merge_call(yn, c, gates, h0, wssd, wcf, wo, g, b, tm):
    t = yn.shape[0]
    rmap = lambda i: (i, 0)
    return pl.pallas_call(
        _merge_kernel,
        grid=(t // tm,),
        in_specs=[pl.BlockSpec((tm, SSD_D_INNER), rmap),
                  pl.BlockSpec((tm, CF_WIDTH), rmap),
                  pl.BlockSpec((tm, 2 * D_MODEL), rmap),
                  pl.BlockSpec((tm, D_MODEL), rmap),
                  _resident((SSD_D_INNER, D_MODEL)),
                  _resident((CF_WIDTH, D_MODEL)),
                  _resident((D_MODEL, D_MODEL)),
                  _resident((1, D_MODEL)),
                  _resident((1, D_MODEL))],
        out_specs=[pl.BlockSpec((tm, D_MODEL), rmap), pl.BlockSpec((tm, D_MODEL), rmap)],
        out_shape=[jax.ShapeDtypeStruct((t, D_MODEL), F32),
                   jax.ShapeDtypeStruct((t, D_MODEL), BF16)],
        compiler_params=_cparams(("parallel",)),
        name="merge",
    )(yn, c, gates, h0, wssd, wcf, wo, g, b)


def _slab(ref, j):
    if isinstance(j, int):
        return ref.at[:, j * PEER_HEADS:(j + 1) * PEER_HEADS, :]
    return ref.at[:, pl.ds(pl.multiple_of(j * PEER_HEADS, PEER_HEADS), PEER_HEADS), :]


def _top16_major(s_ref, val_ref, idx_ref, n):
    neg = jnp.float32(-jnp.inf)
    m0 = _slab(s_ref, 0)[...]
    for j in range(1, n):
        m0 = jnp.maximum(m0, _slab(s_ref, j)[...])

    def body(it, m):
        idx = jnp.full(m.shape, n, jnp.int32)
        for j in range(n):
            idx = jnp.minimum(idx, jnp.where(_slab(s_ref, j)[...] == m, j, n))
        _slab(val_ref, it)[...] = m
        _slab(idx_ref, it)[...] = idx
        nxt = jnp.full(m.shape, neg, F32)
        for j in range(n):
            sj = jnp.where(idx == j, neg, _slab(s_ref, j)[...])
            _slab(s_ref, j)[...] = sj
            nxt = jnp.maximum(nxt, sj)
        return nxt

    lax.fori_loop(0, PEER_TOPK, body, m0)


def _peer_query_kernel(hb_ref, wq_ref, keys_ref, i1_ref, i2_ref, gate_ref,
                       s_ref, v1_ref, x1_ref, v2_ref, x2_ref, cand_ref, bs_ref, e1_ref, e2_ref):
    tq = hb_ref.shape[0]
    lane_blocks = tq // LANES
    q = jnp.dot(hb_ref[...], wq_ref[...], preferred_element_type=F32).astype(BF16)
    half_cols = PEER_DKEY // 2
    for half, (v_ref, x_ref) in enumerate(((v1_ref, x1_ref), (v2_ref, x2_ref))):
        for h in range(PEER_HEADS):
            c0 = h * PEER_DKEY + half * half_cols
            scores = lax.dot_general(keys_ref[half, h], q[:, c0:c0 + half_cols],
                                     (((1,), (1,)), ((), ())), preferred_element_type=F32)
            for lb in range(lane_blocks):
                s_ref[lb, pl.ds(h, PEER_NKEYS, stride=PEER_HEADS), :] = (
                    scores[:, lb * LANES:(lb + 1) * LANES])
        _top16_major(s_ref, v_ref, x_ref, PEER_NKEYS)

    for ci, (a, b) in enumerate(_CAND):
        _slab(cand_ref, ci)[...] = _slab(v1_ref, a)[...] + _slab(v2_ref, b)[...]
    neg = jnp.float32(-jnp.inf)
    m0 = _slab(cand_ref, 0)[...]
    for ci in range(1, len(_CAND)):
        m0 = jnp.maximum(m0, _slab(cand_ref, ci)[...])
    big = PEER_TOPK * PEER_TOPK

    def body(it, m):
        sel = jnp.full(m.shape, big, jnp.int32)
        for ci, (a, b) in enumerate(_CAND):
            sel = jnp.minimum(sel, jnp.where(_slab(cand_ref, ci)[...] == m, a * PEER_TOPK + b, big))
        e1 = jnp.zeros(m.shape, jnp.int32)
        e2 = jnp.zeros(m.shape, jnp.int32)
        nxt = jnp.full(m.shape, neg, F32)
        for ci, (a, b) in enumerate(_CAND):
            hit = sel == a * PEER_TOPK + b
            e1 = jnp.where(hit, _slab(x1_ref, a)[...], e1)
            e2 = jnp.where(hit, _slab(x2_ref, b)[...], e2)
            cj = jnp.where(hit, neg, _slab(cand_ref, ci)[...])
            _slab(cand_ref, ci)[...] = cj
            nxt = jnp.maximum(nxt, cj)
        _slab(bs_ref, it)[...] = m
        _slab(e1_ref, it)[...] = e1
        _slab(e2_ref, it)[...] = e2
        return nxt

    lax.fori_loop(0, PEER_TOPK, body, m0)

    for lb in range(lane_blocks):
        bs = bs_ref[lb].reshape(PEER_TOPK, PEER_HEADS, LANES)
        ex = jnp.exp(bs - bs[0:1])
        gate = ex / jnp.sum(ex, axis=0, keepdims=True)
        tok = slice(lb * LANES, (lb + 1) * LANES)
        gate_ref[tok, :] = gate.reshape(PEER_TOPK * PEER_HEADS, LANES).T
        i1_ref[tok, :] = e1_ref[lb].T
        i2_ref[tok, :] = e2_ref[lb].T


def _peer_query_call(hb, wq, keys, tq):
    t = hb.shape[0]
    nj = PEER_TOPK * PEER_HEADS
    rmap = lambda i: (i, 0)
    slabs = lambda n, dt: pltpu.VMEM((tq // LANES, n * PEER_HEADS, LANES), dt)
    return pl.pallas_call(
        _peer_query_kernel,
        grid=(t // tq,),
        in_specs=[pl.BlockSpec((tq, D_MODEL), rmap),
                  _resident((D_MODEL, PEER_HEADS * PEER_DKEY)),
                  _resident((2, PEER_HEADS, PEER_NKEYS, PEER_DKEY // 2))],
        out_specs=[pl.BlockSpec((tq, nj), rmap)] * 3,
        out_shape=[jax.ShapeDtypeStruct((t, nj), jnp.int32),
                   jax.ShapeDtypeStruct((t, nj), jnp.int32),
                   jax.ShapeDtypeStruct((t, nj), F32)],
        scratch_shapes=[slabs(PEER_NKEYS, F32),
                        slabs(PEER_TOPK, F32), slabs(PEER_TOPK, jnp.int32),
                        slabs(PEER_TOPK, F32), slabs(PEER_TOPK, jnp.int32),
                        slabs(len(_CAND), F32), slabs(PEER_TOPK, F32),
                        slabs(PEER_TOPK, jnp.int32), slabs(PEER_TOPK, jnp.int32)],
        compiler_params=_cparams(("parallel",)),
        name="peer_query",
    )(hb, wq, keys)


BUILD_UNROLL = 16


def _gelu(x):
    return 0.5 * x * (1.0 + lax.erf(x * (1.0 / math.sqrt(2.0))))


def _pack_bf16_pair(a, b):
    bits = lambda v: lax.bitcast_convert_type(v.astype(BF16).astype(F32), jnp.uint32)
    return bits(a) | (bits(b) >> 16)


def _unpack_bf16_pair(w):
    return (lax.bitcast_convert_type(w & jnp.uint32(0xFFFF0000), F32),
            lax.bitcast_convert_type(w << 16, F32))


def _route_rows(i1_row, i2_row, gate_row, key_iota):
    a_t = jnp.where(key_iota == i1_row, gate_row, 0.0).astype(BF16)
    b_t = jnp.where(key_iota == i2_row, 1.0, 0.0).astype(BF16)
    return lax.dot_general(a_t, b_t, (((1,), (1,)), ((), ())), preferred_element_type=F32)


def _peer_expert_kernel(hb_ref, h_ref, i1_ref, i2_ref, gate_ref, u_ref, v_ref, g_ref, b_ref,
                        o_ref, gs_ref, *, tm, te, stride):
    e = pl.program_id(1)
    half = tm // 2

    @pl.when(e == 0)
    def _():
        o_ref[...] = jnp.zeros(o_ref.shape, F32)
        key_iota = lax.broadcasted_iota(jnp.int32, (PEER_NKEYS, PEER_NKEYS), 0)

        def build(blk, carry):
            lo = pl.multiple_of(blk * BUILD_UNROLL, BUILD_UNROLL)
            hi = pl.multiple_of(half + blk * BUILD_UNROLL, BUILD_UNROLL)
            rows = [(r[pl.ds(lo, BUILD_UNROLL), :], r[pl.ds(hi, BUILD_UNROLL), :])
                    for r in (i1_ref, i2_ref, gate_ref)]
            for k in range(BUILD_UNROLL):
                g0 = _route_rows(rows[0][0][k:k + 1], rows[1][0][k:k + 1], rows[2][0][k:k + 1], key_iota)
                g1 = _route_rows(rows[0][1][k:k + 1], rows[1][1][k:k + 1], rows[2][1][k:k + 1], key_iota)
                gs_ref[pl.ds(lo + k, PEER_NKEYS, stride=stride), :] = _pack_bf16_pair(g0, g1)
            return carry

        lax.fori_loop(0, half // BUILD_UNROLL, build, 0)

    s = lax.dot_general(hb_ref[...], u_ref[...], (((1,), (1,)), ((), ())),
                        preferred_element_type=F32)
    slabs = te // PEER_NKEYS
    packed = jnp.concatenate(
        [gs_ref[pl.ds(pl.multiple_of((e * slabs + r) * stride, SUBLANES), half), :]
         for r in range(slabs)], axis=1)
    route = jnp.concatenate(_unpack_bf16_pair(packed), axis=0)
    act = (_gelu(s) * route).astype(BF16)
    o_ref[...] += jnp.dot(act, v_ref[...], preferred_element_type=F32)

    @pl.when(e == pl.num_programs(1) - 1)
    def _():
        o_ref[...] = _layer_norm(DN_ALPHA * h_ref[...] + o_ref[...], g_ref[...], b_ref[...])


def _peer_expert_call(hb, h, i1, i2, gate, u, v, g, b, tm, te):
    t = hb.shape[0]
    nj = PEER_TOPK * PEER_HEADS
    assert tm % (2 * BUILD_UNROLL) == 0
    stride = tm // 2 + SUBLANES
    rmap = lambda i, e: (i, 0)
    emap = lambda i, e: (e, 0)
    cmap = lambda i, e: (0, 0)
    once = dict(pipeline_mode=pl.Buffered(1))
    return pl.pallas_call(
        functools.partial(_peer_expert_kernel, tm=tm, te=te, stride=stride),
        grid=(t // tm, PEER_EXPERTS // te),
        in_specs=[pl.BlockSpec((tm, D_MODEL), rmap, **once),
                  pl.BlockSpec((tm, D_MODEL), rmap, **once),
                  pl.BlockSpec((tm, nj), rmap, **once),
                  pl.BlockSpec((tm, nj), rmap, **once),
                  pl.BlockSpec((tm, nj), rmap, **once),
                  pl.BlockSpec((te, D_MODEL), emap),
                  pl.BlockSpec((te, D_MODEL), emap),
                  pl.BlockSpec((1, D_MODEL), cmap),
                  pl.BlockSpec((1, D_MODEL), cmap)],
        out_specs=pl.BlockSpec((tm, D_MODEL), rmap),
        out_shape=jax.ShapeDtypeStruct((t, D_MODEL), F32),
        scratch_shapes=[pltpu.VMEM((PEER_NKEYS * stride, PEER_NKEYS), jnp.uint32)],
        compiler_params=_cparams(("parallel", "arbitrary"), PEER_VMEM_LIMIT),
        name="peer_experts",
    )(hb, h, i1, i2, gate, u, v, g, b)


def kernel(x, meta, ln0_g, ln0_b, w_in, ssd_conv_w, ssd_conv_b, ssd_dt_bias, ssd_a_log, ssd_d,
           ssd_norm_w, ssd_out, cf_dw_w, cf_dw_b, cf_ln_g, cf_ln_b, cf_out, w_o, ln1_g, ln1_b,
           peer_wq, peer_keys, peer_u, peer_v, ln2_g, ln2_b):
    bsz, seq, d = x.shape
    assert d == D_MODEL and seq % CHUNK == 0
    assert w_in.shape[0] == 1, "single layer"
    t = bsz * seq
    tp = t + bsz * CHUNK
    ln_rows = 2 * CHUNK
    assert t % ln_rows == 0 and tp % ln_rows == 0

    rows = _divisor_tile(seq, 256, CHUNK)
    tm_proj = _divisor_tile(tp, 1664, SUBLANES)
    tm_merge = _divisor_tile(t, 256, SUBLANES)
    tq = _divisor_tile(t, 512, LANES)
    tm_peer = _divisor_tile(t, 512, 2 * BUILD_UNROLL)

    h0, h0b = _ln0_call(x.reshape(t, d), meta, ln0_g, ln0_b, tp, ln_rows)

    w = w_in[0]
    wz = w[:, :OFF_Z].astype(BF16)
    wxbc = w[:, OFF_Z:OFF_XBC].astype(BF16)
    wdt = jnp.pad(w[:, OFF_XBC:OFF_DT], ((0, 0), (0, LANES - SSD_HEADS))).astype(BF16)
    wcv = w[:, OFF_DT:OFF_DT + CF_WIDTH].astype(BF16)
    wcg = w[:, OFF_DT + CF_WIDTH:OFF_CF].astype(BF16)
    wgate = w[:, OFF_CF:].astype(BF16)
    pad_heads = lambda a: jnp.pad(a.astype(F32), (0, LANES - SSD_HEADS)).reshape(1, LANES)
    head_expand = jnp.tile(jnp.arange(SSD_D_INNER)[None, :] // SSD_HEADDIM
                           == jnp.arange(LANES)[:, None], (3, 1)).astype(BF16)

    zs = _proj_call(h0b, [wz], _silu, BF16, tm_proj, 512, name="proj_z")
    xbc = _proj_call(h0b, [wxbc], lambda a: a, BF16, tm_proj, 512, name="proj_xbc")
    dt = _proj_call(h0b, [wdt], _softplus, F32, tm_proj, LANES, bias=pad_heads(ssd_dt_bias[0]),
                    name="proj_dt")
    cglu = _proj_call(h0b, [wcv, wcg], lambda a, g: a * jax.nn.sigmoid(g), BF16, tm_proj, 512,
                      name="proj_glu")
    gates = _proj_call(h0b, [wgate], jax.nn.sigmoid, BF16, tm_proj, 512, name="proj_gate")

    yn = _ssd_call(xbc, zs, dt, ssd_conv_w[0], ssd_conv_b[0].reshape(1, -1), pad_heads(ssd_a_log[0]),
                   jnp.repeat(ssd_d[0].astype(F32), SSD_HEADDIM).reshape(1, -1),
                   ssd_norm_w[0].reshape(1, -1), head_expand, bsz, seq, rows)
    c2 = _conf_call(cglu, cf_dw_w[0], cf_dw_b[0].reshape(1, -1), cf_ln_g[0].reshape(1, -1),
                    cf_ln_b[0].reshape(1, -1), bsz, seq, rows)
    h1, h1b = _merge_call(yn, c2, gates, h0, ssd_out[0].astype(BF16), cf_out[0].astype(BF16),
                          w_o[0].astype(BF16), ln1_g[0].reshape(1, -1), ln1_b[0].reshape(1, -1),
                          tm_merge)

    i1, i2, gate = _peer_query_call(h1b, peer_wq[0].astype(BF16), peer_keys[0].astype(BF16), tq)
    out = _peer_expert_call(h1b, h1, i1, i2, gate, peer_u[0].astype(BF16), peer_v[0].astype(BF16),
                            ln2_g[0].reshape(1, -1), ln2_b[0].reshape(1, -1), tm_peer, 1024)
    return out.reshape(bsz, seq, d)
```

```python
import functools
import math

import jax
import jax.numpy as jnp
from jax import lax
from jax.experimental import pallas as pl
from jax.experimental.pallas import tpu as pltpu

F32 = jnp.float32
BF16 = jnp.bfloat16

D_MODEL = 2048
CHUNK = 64
N_META = 16
PAD_ROWS = CHUNK - N_META
SSD_D_INNER = 2048
SSD_HEADDIM = 64
SSD_HEADS = 32
SSD_GROUPS = 8
SSD_STATE = 128
SSD_CONV = 4
SSD_XBC = SSD_D_INNER + 2 * SSD_GROUPS * SSD_STATE
GROUP_COLS = SSD_D_INNER // SSD_GROUPS
HEADS_PER_GROUP = SSD_HEADS // SSD_GROUPS
CF_WIDTH = 2048
CF_CONV = 31
CF_HALO = 32
PEER_HEADS = 8
PEER_NKEYS = 128
PEER_EXPERTS = PEER_NKEYS * PEER_NKEYS
PEER_DKEY = 256
PEER_TOPK = 16
OFF_Z = SSD_D_INNER
OFF_XBC = OFF_Z + SSD_XBC
OFF_DT = OFF_XBC + SSD_HEADS
OFF_CF = OFF_DT + 2 * CF_WIDTH
LN_EPS = 1e-5
DN_ALPHA = 2.0 ** 0.25
LANES = 128
SUBLANES = 8
VMEM_LIMIT = 56 * 1024 * 1024
PEER_VMEM_LIMIT = 60 * 1024 * 1024

_CAND = [(a, b) for a in range(PEER_TOPK) for b in range(PEER_TOPK)
         if (a + 1) * (b + 1) <= PEER_TOPK]


def _cparams(sem, vmem_limit=VMEM_LIMIT):
    return pltpu.CompilerParams(dimension_semantics=sem, vmem_limit_bytes=vmem_limit)


def _resident(shape):
    nd = len(shape)
    return pl.BlockSpec(shape, lambda *_: (0,) * nd, pipeline_mode=pl.Buffered(1))


def _divisor_tile(n, target, mult):
    best = None
    for t in range(mult, min(n, target) + 1, mult):
        if n % t == 0:
            best = t
    assert best is not None, (n, target, mult)
    return best


def _layer_norm(x, g, b):
    mu = jnp.mean(x, axis=-1, keepdims=True)
    xc = x - mu
    var = jnp.mean(xc * xc, axis=-1, keepdims=True)
    return xc * lax.rsqrt(var + LN_EPS) * g + b


def _silu(x):
    return x * jax.nn.sigmoid(x)


def _ln0_kernel(x_ref, meta_ref, g_ref, b_ref, o_ref, ob_ref, *, n_token_tiles):
    i = pl.program_id(0)

    def emit(rows):
        y = _layer_norm(rows, g_ref[...], b_ref[...])
        o_ref[...] = y
        ob_ref[...] = y.astype(BF16)

    @pl.when(i < n_token_tiles)
    def _():
        emit(x_ref[...])

    @pl.when(i >= n_token_tiles)
    def _():
        head = jnp.concatenate([jnp.zeros((PAD_ROWS, D_MODEL), F32), meta_ref[...]], axis=0)
        emit(jnp.concatenate([head] * (x_ref.shape[0] // CHUNK), axis=0))


def _ln0_call(x2d, meta, g, b, tp, rows):
    t, d = x2d.shape
    n_tok = t // rows
    return pl.pallas_call(
        functools.partial(_ln0_kernel, n_token_tiles=n_tok),
        grid=(pl.cdiv(tp, rows),),
        in_specs=[pl.BlockSpec((rows, d), lambda i: (jnp.minimum(i, n_tok - 1), 0)),
                  pl.BlockSpec((N_META, d), lambda i: (0, 0)),
                  pl.BlockSpec((1, d), lambda i: (0, 0)),
                  pl.BlockSpec((1, d), lambda i: (0, 0))],
        out_specs=[pl.BlockSpec((rows, d), lambda i: (i, 0)),
                   pl.BlockSpec((rows, d), lambda i: (i, 0))],
        out_shape=[jax.ShapeDtypeStruct((tp, d), F32), jax.ShapeDtypeStruct((tp, d), BF16)],
        compiler_params=_cparams(("parallel",)),
        name="ln0",
    )(x2d, meta, g.reshape(1, d), b.reshape(1, d))


def _proj_kernel(x_ref, *refs, n_w, epilogue, has_bias):
    w_refs = refs[:n_w]
    b_ref = refs[n_w] if has_bias else None
    o_ref = refs[-1]
    x = x_ref[...]
    accs = [jnp.dot(x, w[...].reshape(w.shape[-2:]).astype(BF16), preferred_element_type=F32)
            for w in w_refs]
    if has_bias:
        accs[0] = accs[0] + b_ref[...]
    o_ref[...] = epilogue(*accs).astype(o_ref.dtype)


def _proj_call(x, ws, epilogue, out_dtype, tm, tn, bias=None, name="proj", window=None):
    tp, k = x.shape
    in_specs = [pl.BlockSpec((tm, k), lambda i, j: (i, 0))]
    if window is None:
        n = ws[0].shape[1]
        in_specs += [pl.BlockSpec((k, tn), lambda i, j: (0, j)) for _ in ws]
    else:
        col0, n = window
        assert len(ws) == 1 and col0 % tn == 0
        blk0 = col0 // tn
        in_specs.append(pl.BlockSpec((1, k, tn), lambda i, j: (0, 0, blk0 + j)))
    args = [x, *ws]
    if bias is not None:
        in_specs.append(pl.BlockSpec((1, tn), lambda i, j: (0, j)))
        args.append(bias)
    return pl.pallas_call(
        functools.partial(_proj_kernel, n_w=len(ws), epilogue=epilogue, has_bias=bias is not None),
        grid=(tp // tm, n // tn),
        in_specs=in_specs,
        out_specs=pl.BlockSpec((tm, tn), lambda i, j: (i, j)),
        out_shape=jax.ShapeDtypeStruct((tp, n), out_dtype),
        compiler_params=_cparams(("parallel", "parallel")),
        name=name,
    )(*args)


def _softplus(x):
    return jnp.maximum(x, 0.0) + jnp.log1p(jnp.exp(-jnp.abs(x)))


def _split3(x):
    hi = x.astype(BF16)
    r1 = x - hi.astype(F32)
    mid = r1.astype(BF16)
    lo = (r1 - mid.astype(F32)).astype(BF16)
    return jnp.concatenate([hi, mid, lo], axis=1)


def _ssd_block(xin, dt, n, refs, y_ref):
    cw_ref, cb_ref, alog_ref, hexp_ref, ext_ref, state_ref = refs
    hist = SUBLANES
    ext_ref[hist:hist + n, :] = xin
    acc = cb_ref[...] + cw_ref[0:1, :] * ext_ref[pl.ds(hist - (SSD_CONV - 1), n), :]
    for k in range(1, SSD_CONV):
        acc = acc + cw_ref[k:k + 1, :] * ext_ref[pl.ds(hist - (SSD_CONV - 1) + k, n), :]
    ext_ref[0:hist, :] = ext_ref[n:n + hist, :]
    xc = _silu(acc)
    xs = xc[:, :SSD_D_INNER]
    bm = xc[:, SSD_D_INNER:SSD_D_INNER + SSD_GROUPS * SSD_STATE].astype(BF16)
    cm = xc[:, SSD_D_INNER + SSD_GROUPS * SSD_STATE:].astype(BF16)

    da = dt * (-jnp.exp(alog_ref[...]))
    pos = lax.broadcasted_iota(jnp.int32, (n, 1), 0) % CHUNK
    cs = da
    shift = 1
    while shift < CHUNK:
        cs = cs + jnp.where(pos >= shift, pltpu.roll(cs, shift, axis=0), 0.0)
        shift *= 2
    hexp = hexp_ref[...]
    dt_e = jnp.dot(_split3(dt), hexp, preferred_element_type=F32)
    cs_e = jnp.dot(_split3(cs), hexp, preferred_element_type=F32)
    xdt = xs * dt_e

    sub = lax.broadcasted_iota(jnp.int32, (CHUNK, SSD_D_INNER), 0)
    lane_pos = lax.broadcasted_iota(jnp.int32, (CHUNK, SSD_D_INNER), 1) % SSD_HEADDIM
    diag = sub == lane_pos
    causal = sub >= lane_pos
    blk_r = lax.broadcasted_iota(jnp.int32, (GROUP_COLS, GROUP_COLS), 0) // SSD_HEADDIM
    blk_c = lax.broadcasted_iota(jnp.int32, (GROUP_COLS, GROUP_COLS), 1) // SSD_HEADDIM
    same_head = blk_r == blk_c

    for c in range(n // CHUNK):
        sl = slice(c * CHUNK, (c + 1) * CHUNK)
        cs_c = cs_e[sl]
        cs_end = cs_c[CHUNK - 1:CHUNK, :]
        decay_out = jnp.exp(cs_end - cs_c)
        decay_chunk = jnp.exp(cs_end)
        if y_ref is not None:
            cs_row = jnp.sum(jnp.where(diag, cs_c, 0.0), axis=0, keepdims=True)
            decay_l = jnp.where(causal, jnp.exp(cs_c - cs_row), 0.0)
            decay_in = jnp.exp(cs_c)
        for g in range(SSD_GROUPS):
            gc = slice(g * GROUP_COLS, (g + 1) * GROUP_COLS)
            gn = slice(g * SSD_STATE, (g + 1) * SSD_STATE)
            bm_g = bm[sl, gn]
            x_g = xdt[sl, gc]
            st = state_ref[g]
            if y_ref is not None:
                cm_g = cm[sl, gn]
                bm_rep = jnp.concatenate([bm_g] * HEADS_PER_GROUP, axis=0)
                scores = lax.dot_general(cm_g, bm_rep, (((1,), (1,)), ((), ())),
                                         preferred_element_type=F32)
                m = (scores * decay_l[:, gc]).astype(BF16)
                x_rep = jnp.concatenate([x_g] * HEADS_PER_GROUP, axis=0)
                x_bd = jnp.where(same_head, x_rep, 0.0).astype(BF16)
                y_diag = jnp.dot(m, x_bd, preferred_element_type=F32)
                y_off = jnp.dot(cm_g, st.astype(BF16), preferred_element_type=F32) * decay_in[:, gc]
                y_ref[sl, gc] = y_diag + y_off
            xd = (x_g * decay_out[:, gc]).astype(BF16)
            upd = lax.dot_general(bm_g, xd, (((0,), (0,)), ((), ())), preferred_element_type=F32)
            state_ref[g] = st * decay_chunk[:, gc] + upd
    return xs


def _ssd_kernel(xbc_ref, zs_ref, dt_ref, xbc_head_ref, dt_head_ref, cw_ref, cb_ref, alog_ref,
                dskip_ref, nw_ref, hexp_ref, o_ref, ext_ref, state_ref, y_ref, *, rows):
    refs = (cw_ref, cb_ref, alog_ref, hexp_ref, ext_ref, state_ref)

    @pl.when(pl.program_id(1) == 0)
    def _():
        ext_ref[0:SUBLANES, :] = jnp.zeros((SUBLANES, SSD_XBC), F32)
        state_ref[...] = jnp.zeros(state_ref.shape, F32)
        meta_row = lax.broadcasted_iota(jnp.int32, (CHUNK, 1), 0) >= PAD_ROWS
        _ssd_block(jnp.where(meta_row, xbc_head_ref[...].astype(F32), 0.0),
                   jnp.where(meta_row, dt_head_ref[...], 0.0), CHUNK, refs, None)

    xs = _ssd_block(xbc_ref[...].astype(F32), dt_ref[...], rows, refs, y_ref)
    y = y_ref[...] + dskip_ref[...] * xs
    yf = y * zs_ref[...].astype(F32)
    for g in range(SSD_GROUPS):
        gc = slice(g * GROUP_COLS, (g + 1) * GROUP_COLS)
        seg = yf[:, gc]
        ms = jnp.mean(seg * seg, axis=-1, keepdims=True)
        o_ref[:, gc] = (seg * lax.rsqrt(ms + LN_EPS) * nw_ref[:, gc]).astype(o_ref.dtype)


def _ssd_call(xbc, zs, dt, cw, cb, alog, dskip, nw, hexp, bsz, seq, rows):
    nblk = seq // rows
    head0 = bsz * seq // CHUNK
    rmap = lambda b, i: (b * nblk + i, 0)
    hmap = lambda b, i: (head0 + b, 0)
    cmap = lambda b, i: (0, 0)
    return pl.pallas_call(
        functools.partial(_ssd_kernel, rows=rows),
        grid=(bsz, nblk),
        in_specs=[pl.BlockSpec((rows, SSD_XBC), rmap),
                  pl.BlockSpec((rows, SSD_D_INNER), rmap),
                  pl.BlockSpec((rows, LANES), rmap),
                  pl.BlockSpec((CHUNK, SSD_XBC), hmap),
                  pl.BlockSpec((CHUNK, LANES), hmap),
                  pl.BlockSpec((SSD_CONV, SSD_XBC), cmap),
                  pl.BlockSpec((1, SSD_XBC), cmap),
                  pl.BlockSpec((1, LANES), cmap),
                  pl.BlockSpec((1, SSD_D_INNER), cmap),
                  pl.BlockSpec((1, SSD_D_INNER), cmap),
                  pl.BlockSpec((3 * LANES, SSD_D_INNER), cmap)],
        out_specs=pl.BlockSpec((rows, SSD_D_INNER), rmap),
        out_shape=jax.ShapeDtypeStruct((bsz * seq, SSD_D_INNER), BF16),
        scratch_shapes=[pltpu.VMEM((SUBLANES + rows, SSD_XBC), F32),
                        pltpu.VMEM((SSD_GROUPS, SSD_STATE, GROUP_COLS), F32),
                        pltpu.VMEM((rows, SSD_D_INNER), F32)],
        compiler_params=_cparams(("arbitrary", "arbitrary")),
        name="ssd",
    )(xbc, zs, dt, xbc, dt, cw, cb, alog, dskip, nw, hexp)


CF_ROW_BLK = 64
CF_COL_BLK = 256


def _glu_proj(x, wv_ref, wg_ref, cols):
    val = jnp.dot(x, wv_ref[:, cols], preferred_element_type=F32)
    gate = jnp.dot(x, wg_ref[:, cols], preferred_element_type=F32)
    return val * jax.nn.sigmoid(gate)


def _conf_kernel(x0_ref, xnext_ref, xhead_ref, wv_ref, wg_ref, w_ref, b_ref, g_ref, beta_ref, o_ref,
                 ext_ref, sh_ref, acc_ref, cnext_ref, *, rows):
    col_blocks = CF_WIDTH // CF_COL_BLK

    @pl.when(pl.program_id(1) == 0)
    def _():
        meta_row = lax.broadcasted_iota(jnp.int32, (CF_HALO, 1), 0) >= CF_HALO - N_META
        x_tail = xhead_ref[CHUNK - CF_HALO:CHUNK, :]
        for cb in range(col_blocks):
            cols = slice(cb * CF_COL_BLK, (cb + 1) * CF_COL_BLK)
            ext_ref[0:CF_HALO, cols] = jnp.where(meta_row, _glu_proj(x_tail, wv_ref, wg_ref, cols), 0.0)
            cnext_ref[:, cols] = _glu_proj(x0_ref[...], wv_ref, wg_ref, cols)

    ext_ref[CF_HALO:CF_HALO + rows, :] = cnext_ref[...]
    span = rows + CF_HALO - SUBLANES
    for s in range(1, SUBLANES):
        sh_ref[s - 1] = ext_ref[pl.ds(s, span), :]
    base = CF_HALO - (CF_CONV - 1)

    def col_block(cb, carry):
        cols = pl.ds(pl.multiple_of(cb * CF_COL_BLK, CF_COL_BLK), CF_COL_BLK)
        nxt = _glu_proj(xnext_ref[...], wv_ref, wg_ref, cols)
        for rb in range(rows // CF_ROW_BLK):
            r0 = rb * CF_ROW_BLK
            acc = jnp.broadcast_to(b_ref[:, cols], (CF_ROW_BLK, CF_COL_BLK))
            for k in range(CF_CONV):
                s = (base + k) % SUBLANES
                q = r0 + (base + k - s)
                if s == 0:
                    src = ext_ref[q:q + CF_ROW_BLK, cols]
                else:
                    src = sh_ref[s - 1, q:q + CF_ROW_BLK, cols]
                acc = acc + w_ref[k:k + 1, cols] * src
            acc_ref[r0:r0 + CF_ROW_BLK, cols] = acc
        cnext_ref[:, cols] = nxt
        return carry

    lax.fori_loop(0, col_blocks, col_block, 0)
    ext_ref[0:CF_HALO, :] = ext_ref[rows:rows + CF_HALO, :]
    o_ref[...] = _silu(_layer_norm(acc_ref[...], g_ref[...], beta_ref[...])).astype(o_ref.dtype)


def _conf_call(hb, wv, wg, w, b, g, beta, bsz, seq, rows):
    nblk = seq // rows
    head0 = bsz * seq // CHUNK
    rmap = lambda bb, i: (bb * nblk + i, 0)
    first = lambda bb, i: (bb * nblk, 0)
    nxt = lambda bb, i: (bb * nblk + jnp.minimum(i + 1, nblk - 1), 0)
    hmap = lambda bb, i: (head0 + bb, 0)
    cmap = lambda bb, i: (0, 0)
    return pl.pallas_call(
        functools.partial(_conf_kernel, rows=rows),
        grid=(bsz, nblk),
        in_specs=[pl.BlockSpec((rows, D_MODEL), first),
                  pl.BlockSpec((rows, D_MODEL), nxt),
                  pl.BlockSpec((CHUNK, D_MODEL), hmap),
                  _resident((D_MODEL, CF_WIDTH)),
                  _resident((D_MODEL, CF_WIDTH)),
                  pl.BlockSpec((CF_CONV, CF_WIDTH), cmap),
                  pl.BlockSpec((1, CF_WIDTH), cmap),
                  pl.BlockSpec((1, CF_WIDTH), cmap),
                  pl.BlockSpec((1, CF_WIDTH), cmap)],
        out_specs=pl.BlockSpec((rows, CF_WIDTH), rmap),
        out_shape=jax.ShapeDtypeStruct((bsz * seq, CF_WIDTH), BF16),
        scratch_shapes=[pltpu.VMEM((CF_HALO + rows, CF_WIDTH), F32),
                        pltpu.VMEM((SUBLANES - 1, CF_HALO + rows - SUBLANES, CF_WIDTH), F32),
                        pltpu.VMEM((rows, CF_WIDTH), F32),
                        pltpu.VMEM((rows, CF_WIDTH), F32)],
        compiler_params=_cparams(("arbitrary", "arbitrary")),
        name="conformer",
    )(hb, hb, hb, wv, wg, w, b, g, beta)


def _merge_kernel(yn_ref, c_ref, gate_ref, h_ref, wssd_ref, wcf_ref, wo_ref, g_ref, b_ref,
                  o_ref, ob_ref):
    y_ssd = jnp.dot(yn_ref[...], wssd_ref[...], preferred_element_type=F32)
    y_cf = jnp.dot(c_ref[...], wcf_ref[...], preferred_element_type=F32)
    gates = gate_ref[...].astype(F32)
    mix = gates[:, :D_MODEL] * y_ssd + gates[:, D_MODEL:] * y_cf
    m = jnp.dot(mix.astype(BF16), wo_ref[...], preferred_element_type=F32)
    h1 = _layer_norm(DN_ALPHA * h_ref[...] + m, g_ref[...], b_ref[...])
    o_ref[...] = h1
    ob_ref[...] = h1.astype(BF16)


def _merge_call(yn, c, gates, h0, wssd, wcf, wo, g, b, tm):
    t = yn.shape[0]
    rmap = lambda i: (i, 0)
    return pl.pallas_call(
        _merge_kernel,
        grid=(t // tm,),
        in_specs=[pl.BlockSpec((tm, SSD_D_INNER), rmap),
                  pl.BlockSpec((tm, CF_WIDTH), rmap),
                  pl.BlockSpec((tm, 2 * D_MODEL), rmap),
                  pl.BlockSpec((tm, D_MODEL), rmap),
                  _resident((SSD_D_INNER, D_MODEL)),
                  _resident((CF_WIDTH, D_MODEL)),
                  _resident((D_MODEL, D_MODEL)),
                  _resident((1, D_MODEL)),
                  _resident((1, D_MODEL))],
        out_specs=[pl.BlockSpec((tm, D_MODEL), rmap), pl.BlockSpec((tm, D_MODEL), rmap)],
        out_shape=[jax.ShapeDtypeStruct((t, D_MODEL), F32),
                   jax.ShapeDtypeStruct((t, D_MODEL), BF16)],
        compiler_params=_cparams(("parallel",)),
        name="merge",
    )(yn, c, gates, h0, wssd, wcf, wo, g, b)


def _slab(ref, j):
    if isinstance(j, int):
        return ref.at[:, j * PEER_HEADS:(j + 1) * PEER_HEADS, :]
    return ref.at[:, pl.ds(pl.multiple_of(j * PEER_HEADS, PEER_HEADS), PEER_HEADS), :]


def _top16_major(s_ref, val_ref, idx_ref, n):
    neg = jnp.float32(-jnp.inf)
    m0 = _slab(s_ref, 0)[...]
    for j in range(1, n):
        m0 = jnp.maximum(m0, _slab(s_ref, j)[...])

    def body(it, m):
        idx = jnp.full(m.shape, n, jnp.int32)
        for j in range(n):
            idx = jnp.minimum(idx, jnp.where(_slab(s_ref, j)[...] == m, j, n))
        _slab(val_ref, it)[...] = m
        _slab(idx_ref, it)[...] = idx
        nxt = jnp.full(m.shape, neg, F32)
        for j in range(n):
            sj = jnp.where(idx == j, neg, _slab(s_ref, j)[...])
            _slab(s_ref, j)[...] = sj
            nxt = jnp.maximum(nxt, sj)
        return nxt

    lax.fori_loop(0, PEER_TOPK, body, m0)


def _peer_query_kernel(hb_ref, wq_ref, keys_ref, i1_ref, i2_ref, gate_ref,
                       s_ref, v1_ref, x1_ref, v2_ref, x2_ref, cand_ref, bs_ref, e1_ref, e2_ref):
    tq = hb_ref.shape[0]
    lane_blocks = tq // LANES
    q = jnp.dot(hb_ref[...], wq_ref[...], preferred_element_type=F32).astype(BF16)
    half_cols = PEER_DKEY // 2
    for half, (v_ref, x_ref) in enumerate(((v1_ref, x1_ref), (v2_ref, x2_ref))):
        for h in range(PEER_HEADS):
            c0 = h * PEER_DKEY + half * half_cols
            scores = lax.dot_general(keys_ref[half, h], q[:, c0:c0 + half_cols],
                                     (((1,), (1,)), ((), ())), preferred_element_type=F32)
            for lb in range(lane_blocks):
                s_ref[lb, pl.ds(h, PEER_NKEYS, stride=PEER_HEADS), :] = (
                    scores[:, lb * LANES:(lb + 1) * LANES])
        _top16_major(s_ref, v_ref, x_ref, PEER_NKEYS)

    for ci, (a, b) in enumerate(_CAND):
        _slab(cand_ref, ci)[...] = _slab(v1_ref, a)[...] + _slab(v2_ref, b)[...]
    neg = jnp.float32(-jnp.inf)
    m0 = _slab(cand_ref, 0)[...]
    for ci in range(1, len(_CAND)):
        m0 = jnp.maximum(m0, _slab(cand_ref, ci)[...])
    big = PEER_TOPK * PEER_TOPK

    def body(it, m):
        sel = jnp.full(m.shape, big, jnp.int32)
        for ci, (a, b) in enumerate(_CAND):
            sel = jnp.minimum(sel, jnp.where(_slab(cand_ref, ci)[...] == m, a * PEER_TOPK + b, big))
        e1 = jnp.zeros(m.shape, jnp.int32)
        e2 = jnp.zeros(m.shape, jnp.int32)
        nxt = jnp.full(m.shape, neg, F32)
        for ci, (a, b) in enumerate(_CAND):
            hit = sel == a * PEER_TOPK + b
            e1 = jnp.where(hit, _slab(x1_ref, a)[...], e1)
            e2 = jnp.where(hit, _slab(x2_ref, b)[...], e2)
            cj = jnp.where(hit, neg, _slab(cand_ref, ci)[...])
            _slab(cand_ref, ci)[...] = cj
            nxt = jnp.maximum(nxt, cj)
        _slab(bs_ref, it)[...] = m
        _slab(e1_ref, it)[...] = e1
        _slab(e2_ref, it)[...] = e2
        return nxt

    lax.fori_loop(0, PEER_TOPK, body, m0)

    for lb in range(lane_blocks):
        bs = bs_ref[lb].reshape(PEER_TOPK, PEER_HEADS, LANES)
        ex = jnp.exp(bs - bs[0:1])
        gate = ex / jnp.sum(ex, axis=0, keepdims=True)
        tok = slice(lb * LANES, (lb + 1) * LANES)
        gate_ref[tok, :] = gate.reshape(PEER_TOPK * PEER_HEADS, LANES).T
        i1_ref[tok, :] = e1_ref[lb].T
        i2_ref[tok, :] = e2_ref[lb].T


def _peer_query_call(hb, wq, keys, tq):
    t = hb.shape[0]
    nj = PEER_TOPK * PEER_HEADS
    rmap = lambda i: (i, 0)
    slabs = lambda n, dt: pltpu.VMEM((tq // LANES, n * PEER_HEADS, LANES), dt)
    return pl.pallas_call(
        _peer_query_kernel,
        grid=(t // tq,),
        in_specs=[pl.BlockSpec((tq, D_MODEL), rmap),
                  _resident((D_MODEL, PEER_HEADS * PEER_DKEY)),
                  _resident((2, PEER_HEADS, PEER_NKEYS, PEER_DKEY // 2))],
        out_specs=[pl.BlockSpec((tq, nj), rmap)] * 3,
        out_shape=[jax.ShapeDtypeStruct((t, nj), jnp.int32),
                   jax.ShapeDtypeStruct((t, nj), jnp.int32),
                   jax.ShapeDtypeStruct((t, nj), F32)],
        scratch_shapes=[slabs(PEER_NKEYS, F32),
                        slabs(PEER_TOPK, F32), slabs(PEER_TOPK, jnp.int32),
                        slabs(PEER_TOPK, F32), slabs(PEER_TOPK, jnp.int32),
                        slabs(len(_CAND), F32), slabs(PEER_TOPK, F32),
                        slabs(PEER_TOPK, jnp.int32), slabs(PEER_TOPK, jnp.int32)],
        compiler_params=_cparams(("parallel",)),
        name="peer_query",
    )(hb, wq, keys)


BUILD_UNROLL = 16


def _gelu(x):
    return 0.5 * x * (1.0 + lax.erf(x * (1.0 / math.sqrt(2.0))))


def _pack_bf16_pair(a, b):
    bits = lambda v: lax.bitcast_convert_type(v, jnp.uint32)
    return (bits(a) & jnp.uint32(0xFFFF0000)) | (bits(b) >> 16)


def _unpack_bf16_pair(w):
    return (lax.bitcast_convert_type(w & jnp.uint32(0xFFFF0000), F32),
            lax.bitcast_convert_type(w << 16, F32))


def _route_rows(i1_row, i2_row, gate_row, key_iota):
    a_t = jnp.where(key_iota == i1_row, gate_row, 0.0).astype(BF16)
    b_t = jnp.where(key_iota == i2_row, 1.0, 0.0).astype(BF16)
    return lax.dot_general(a_t, b_t, (((1,), (1,)), ((), ())), preferred_element_type=F32)


def _peer_expert_kernel(hb_ref, h_ref, i1_ref, i2_ref, gate_ref, u_ref, v_ref, g_ref, b_ref,
                        o_ref, gs_ref, *, tm, te, stride):
    e = pl.program_id(1)
    half = tm // 2

    @pl.when(e == 0)
    def _():
        o_ref[...] = jnp.zeros(o_ref.shape, F32)
        key_iota = lax.broadcasted_iota(jnp.int32, (PEER_NKEYS, PEER_NKEYS), 0)

        def build(blk, carry):
            lo = pl.multiple_of(blk * BUILD_UNROLL, BUILD_UNROLL)
            hi = pl.multiple_of(half + blk * BUILD_UNROLL, BUILD_UNROLL)
            rows = [(r[pl.ds(lo, BUILD_UNROLL), :], r[pl.ds(hi, BUILD_UNROLL), :])
                    for r in (i1_ref, i2_ref, gate_ref)]
            for k in range(BUILD_UNROLL):
                g0 = _route_rows(rows[0][0][k:k + 1], rows[1][0][k:k + 1], rows[2][0][k:k + 1], key_iota)
                g1 = _route_rows(rows[0][1][k:k + 1], rows[1][1][k:k + 1], rows[2][1][k:k + 1], key_iota)
                gs_ref[pl.ds(lo + k, PEER_NKEYS, stride=stride), :] = _pack_bf16_pair(g0, g1)
            return carry

        lax.fori_loop(0, half // BUILD_UNROLL, build, 0)

    s = lax.dot_general(hb_ref[...], u_ref[...], (((1,), (1,)), ((), ())),
                        preferred_element_type=F32)
    slabs = te // PEER_NKEYS
    packed = jnp.concatenate(
        [gs_ref[pl.ds(pl.multiple_of((e * slabs + r) * stride, SUBLANES), half), :]
         for r in range(slabs)], axis=1)
    route = jnp.concatenate(_unpack_bf16_pair(packed), axis=0)
    act = (_gelu(s) * route).astype(BF16)
    o_ref[...] += jnp.dot(act, v_ref[...], preferred_element_type=F32)

    @pl.when(e == pl.num_programs(1) - 1)
    def _():
        o_ref[...] = _layer_norm(DN_ALPHA * h_ref[...] + o_ref[...], g_ref[...], b_ref[...])


def _peer_expert_call(hb, h, i1, i2, gate, u, v, g, b, tm, te):
    t = hb.shape[0]
    nj = PEER_TOPK * PEER_HEADS
    assert tm % (2 * BUILD_UNROLL) == 0
    stride = tm // 2 + SUBLANES
    rmap = lambda i, e: (i, 0)
    emap = lambda i, e: (e, 0)
    cmap = lambda i, e: (0, 0)
    once = dict(pipeline_mode=pl.Buffered(1))
    return pl.pallas_call(
        functools.partial(_peer_expert_kernel, tm=tm, te=te, stride=stride),
        grid=(t // tm, PEER_EXPERTS // te),
        in_specs=[pl.BlockSpec((tm, D_MODEL), rmap, **once),
                  pl.BlockSpec((tm, D_MODEL), rmap, **once),
                  pl.BlockSpec((tm, nj), rmap, **once),
                  pl.BlockSpec((tm, nj), rmap, **once),
                  pl.BlockSpec((tm, nj), rmap, **once),
                  pl.BlockSpec((te, D_MODEL), emap),
                  pl.BlockSpec((te, D_MODEL), emap),
                  pl.BlockSpec((1, D_MODEL), cmap),
                  pl.BlockSpec((1, D_MODEL), cmap)],
        out_specs=pl.BlockSpec((tm, D_MODEL), rmap),
        out_shape=jax.ShapeDtypeStruct((t, D_MODEL), F32),
        scratch_shapes=[pltpu.VMEM((PEER_NKEYS * stride, PEER_NKEYS), jnp.uint32)],
        compiler_params=_cparams(("parallel", "arbitrary"), PEER_VMEM_LIMIT),
        name="peer_experts",
    )(hb, h, i1, i2, gate, u, v, g, b)


def kernel(x, meta, ln0_g, ln0_b, w_in, ssd_conv_w, ssd_conv_b, ssd_dt_bias, ssd_a_log, ssd_d,
           ssd_norm_w, ssd_out, cf_dw_w, cf_dw_b, cf_ln_g, cf_ln_b, cf_out, w_o, ln1_g, ln1_b,
           peer_wq, peer_keys, peer_u, peer_v, ln2_g, ln2_b):
    bsz, seq, d = x.shape
    assert d == D_MODEL and seq % CHUNK == 0
    assert w_in.shape[0] == 1, "single layer"
    t = bsz * seq
    tp = t + bsz * CHUNK
    ln_rows = _divisor_tile(t, 512, 2 * CHUNK)
    assert bsz * CHUNK <= ln_rows

    rows = _divisor_tile(seq, 256, CHUNK)
    tm_proj = _divisor_tile(tp, 1664, SUBLANES)
    tm_merge = _divisor_tile(t, 256, SUBLANES)
    tq = _divisor_tile(t, 512, LANES)
    tm_peer = _divisor_tile(t, 512, 2 * BUILD_UNROLL)

    h0, h0b = _ln0_call(x.reshape(t, d), meta, ln0_g, ln0_b, tp, ln_rows)

    w = w_in[0]
    wcv = w[:, OFF_DT:OFF_DT + CF_WIDTH].astype(BF16)
    wcg = w[:, OFF_DT + CF_WIDTH:OFF_CF].astype(BF16)
    wgate = w[:, OFF_CF:].astype(BF16)
    pad_heads = lambda a: jnp.pad(a.astype(F32), (0, LANES - SSD_HEADS)).reshape(1, LANES)
    head_expand = jnp.tile(jnp.arange(SSD_D_INNER)[None, :] // SSD_HEADDIM
                           == jnp.arange(LANES)[:, None], (3, 1)).astype(BF16)

    zs = _proj_call(h0b, [w_in], _silu, BF16, tm_proj, 512, name="proj_z", window=(0, OFF_Z))
    xbc = _proj_call(h0b, [w_in], lambda a: a, BF16, tm_proj, 512, name="proj_xbc",
                     window=(OFF_Z, SSD_XBC))
    dt = _proj_call(h0b, [w_in], _softplus, F32, tm_proj, LANES, bias=pad_heads(ssd_dt_bias[0]),
                    name="proj_dt", window=(OFF_XBC, LANES))
    gates =_proj_call(h0b, [wgate], jax.nn.sigmoid, BF16, tm_proj, 512, name="proj_gate")

    yn = _ssd_call(xbc, zs, dt, ssd_conv_w[0], ssd_conv_b[0].reshape(1, -1), pad_heads(ssd_a_log[0]),
                   jnp.repeat(ssd_d[0].astype(F32), SSD_HEADDIM).reshape(1, -1),
                   ssd_norm_w[0].reshape(1, -1), head_expand, bsz, seq, rows)
    c2 = _conf_call(h0b, wcv, wcg, cf_dw_w[0], cf_dw_b[0].reshape(1, -1), cf_ln_g[0].reshape(1, -1),
                    cf_ln_b[0].reshape(1, -1), bsz, seq, rows)
    h1, h1b = _merge_call(yn, c2, gates, h0, ssd_out[0].astype(BF16), cf_out[0].astype(BF16),
                          w_o[0].astype(BF16), ln1_g[0].reshape(1, -1), ln1_b[0].reshape(1, -1),
                          tm_merge)

    i1, i2, gate = _peer_query_call(h1b, peer_wq[0].astype(BF16), peer_keys[0].astype(BF16), tq)
    out = _peer_expert_call(h1b, h1, i1, i2, gate, peer_u[0].astype(BF16), peer_v[0].astype(BF16),
                            ln2_g[0].reshape(1, -1), ln2_b[0].reshape(1, -1), tm_peer, 1024)
    return out.reshape(bsz, seq, d)
```

```python
import functools
import math

import jax
import jax.numpy as jnp
from jax import lax
from jax.experimental import pallas as pl
from jax.experimental.pallas import tpu as pltpu

F32 = jnp.float32
BF16 = jnp.bfloat16

D_MODEL = 2048
CHUNK = 64
N_META = 16
PAD_ROWS = CHUNK - N_META
SSD_D_INNER = 2048
SSD_HEADDIM = 64
SSD_HEADS = 32
SSD_GROUPS = 8
SSD_STATE = 128
SSD_CONV = 4
SSD_XBC = SSD_D_INNER + 2 * SSD_GROUPS * SSD_STATE
GROUP_COLS = SSD_D_INNER // SSD_GROUPS
HEADS_PER_GROUP = SSD_HEADS // SSD_GROUPS
CF_WIDTH = 2048
CF_CONV = 31
CF_HALO = 32
PEER_HEADS = 8
PEER_NKEYS = 128
PEER_EXPERTS = PEER_NKEYS * PEER_NKEYS
PEER_DKEY = 256
PEER_TOPK = 16
OFF_Z = SSD_D_INNER
OFF_XBC = OFF_Z + SSD_XBC
OFF_DT = OFF_XBC + SSD_HEADS
OFF_CF = OFF_DT + 2 * CF_WIDTH
LN_EPS = 1e-5
DN_ALPHA = 2.0 ** 0.25
LANES = 128
SUBLANES = 8
VMEM_LIMIT = 56 * 1024 * 1024
PEER_VMEM_LIMIT = 60 * 1024 * 1024

_CAND = [(a, b) for a in range(PEER_TOPK) for b in range(PEER_TOPK)
         if (a + 1) * (b + 1) <= PEER_TOPK]


def _cparams(sem, vmem_limit=VMEM_LIMIT):
    return pltpu.CompilerParams(dimension_semantics=sem, vmem_limit_bytes=vmem_limit)


def _resident(shape):
    nd = len(shape)
    return pl.BlockSpec(shape, lambda *_: (0,) * nd, pipeline_mode=pl.Buffered(1))


def _divisor_tile(n, target, mult):
    best = None
    for t in range(mult, min(n, target) + 1, mult):
        if n % t == 0:
            best = t
    assert best is not None, (n, target, mult)
    return best


def _layer_norm(x, g, b):
    mu = jnp.mean(x, axis=-1, keepdims=True)
    xc = x - mu
    var = jnp.mean(xc * xc, axis=-1, keepdims=True)
    return xc * lax.rsqrt(var + LN_EPS) * g + b


def _silu(x):
    return x * jax.nn.sigmoid(x)


def _ln0_kernel(x_ref, meta_ref, g_ref, b_ref, o_ref, ob_ref, *, n_token_tiles):
    i = pl.program_id(0)

    def emit(rows):
        y = _layer_norm(rows, g_ref[...], b_ref[...])
        o_ref[...] = y
        ob_ref[...] = y.astype(BF16)

    @pl.when(i < n_token_tiles)
    def _():
        emit(x_ref[...])

    @pl.when(i >= n_token_tiles)
    def _():
        head = jnp.concatenate([jnp.zeros((PAD_ROWS, D_MODEL), F32), meta_ref[...]], axis=0)
        emit(jnp.concatenate([head] * (x_ref.shape[0] // CHUNK), axis=0))


def _ln0_call(x2d, meta, g, b, tp, rows):
    t, d = x2d.shape
    n_tok = t // rows
    return pl.pallas_call(
        functools.partial(_ln0_kernel, n_token_tiles=n_tok),
        grid=(pl.cdiv(tp, rows),),
        in_specs=[pl.BlockSpec((rows, d), lambda i: (jnp.minimum(i, n_tok - 1), 0)),
                  pl.BlockSpec((N_META, d), lambda i: (0, 0)),
                  pl.BlockSpec((1, d), lambda i: (0, 0)),
                  pl.BlockSpec((1, d), lambda i: (0, 0))],
        out_specs=[pl.BlockSpec((rows, d), lambda i: (i, 0)),
                   pl.BlockSpec((rows, d), lambda i: (i, 0))],
        out_shape=[jax.ShapeDtypeStruct((tp, d), F32), jax.ShapeDtypeStruct((tp, d), BF16)],
        compiler_params=_cparams(("parallel",)),
        name="ln0",
    )(x2d, meta, g.reshape(1, d), b.reshape(1, d))


def _proj_kernel(x_ref, *refs, n_w, epilogue, has_bias):
    w_refs = refs[:n_w]
    b_ref = refs[n_w] if has_bias else None
    o_ref = refs[-1]
    x = x_ref[...]
    accs = [jnp.dot(x, w[...], preferred_element_type=F32) for w in w_refs]
    if has_bias:
        accs[0] = accs[0] + b_ref[...]
    o_ref[...] = epilogue(*accs).astype(o_ref.dtype)


def _proj_call(x, ws, epilogue, out_dtype, tm, tn, bias=None, name="proj"):
    tp, k = x.shape
    n = ws[0].shape[1]
    in_specs = [pl.BlockSpec((tm, k), lambda i, j: (i, 0))]
    in_specs += [pl.BlockSpec((k, tn), lambda i, j: (0, j)) for _ in ws]
    args = [x, *ws]
    if bias is not None:
        in_specs.append(pl.BlockSpec((1, tn), lambda i, j: (0, j)))
        args.append(bias)
    return pl.pallas_call(
        functools.partial(_proj_kernel, n_w=len(ws), epilogue=epilogue, has_bias=bias is not None),
        grid=(tp // tm, n // tn),
        in_specs=in_specs,
        out_specs=pl.BlockSpec((tm, tn), lambda i, j: (i, j)),
        out_shape=jax.ShapeDtypeStruct((tp, n), out_dtype),
        compiler_params=_cparams(("parallel", "parallel")),
        name=name,
    )(*args)


def _softplus(x):
    return jnp.maximum(x, 0.0) + jnp.log1p(jnp.exp(-jnp.abs(x)))


def _split3(x):
    hi = x.astype(BF16)
    r1 = x - hi.astype(F32)
    mid = r1.astype(BF16)
    lo = (r1 - mid.astype(F32)).astype(BF16)
    return jnp.concatenate([hi, mid, lo], axis=1)


def _ssd_block(xin, dt, n, refs, y_ref):
    cw_ref, cb_ref, alog_ref, hexp_ref, ext_ref, state_ref = refs
    hist = SUBLANES
    ext_ref[hist:hist + n, :] = xin
    acc = cb_ref[...] + cw_ref[0:1, :] * ext_ref[pl.ds(hist - (SSD_CONV - 1), n), :]
    for k in range(1, SSD_CONV):
        acc = acc + cw_ref[k:k + 1, :] * ext_ref[pl.ds(hist - (SSD_CONV - 1) + k, n), :]
    ext_ref[0:hist, :] = ext_ref[n:n + hist, :]
    xc = _silu(acc)
    xs = xc[:, :SSD_D_INNER]
    bm = xc[:, SSD_D_INNER:SSD_D_INNER + SSD_GROUPS * SSD_STATE].astype(BF16)
    cm = xc[:, SSD_D_INNER + SSD_GROUPS * SSD_STATE:].astype(BF16)

    da = dt * (-jnp.exp(alog_ref[...]))
    pos = lax.broadcasted_iota(jnp.int32, (n, 1), 0) % CHUNK
    cs = da
    shift = 1
    while shift < CHUNK:
        cs = cs + jnp.where(pos >= shift, pltpu.roll(cs, shift, axis=0), 0.0)
        shift *= 2
    hexp = hexp_ref[...]
    dt_e = jnp.dot(_split3(dt), hexp, preferred_element_type=F32)
    cs_e = jnp.dot(_split3(cs), hexp, preferred_element_type=F32)
    xdt = xs * dt_e

    sub = lax.broadcasted_iota(jnp.int32, (CHUNK, SSD_D_INNER), 0)
    lane_pos = lax.broadcasted_iota(jnp.int32, (CHUNK, SSD_D_INNER), 1) % SSD_HEADDIM
    diag = sub == lane_pos
    causal = sub >= lane_pos
    blk_r = lax.broadcasted_iota(jnp.int32, (GROUP_COLS, GROUP_COLS), 0) // SSD_HEADDIM
    blk_c = lax.broadcasted_iota(jnp.int32, (GROUP_COLS, GROUP_COLS), 1) // SSD_HEADDIM
    same_head = blk_r == blk_c

    for c in range(n // CHUNK):
        sl = slice(c * CHUNK, (c + 1) * CHUNK)
        cs_c = cs_e[sl]
        cs_end = cs_c[CHUNK - 1:CHUNK, :]
        decay_out = jnp.exp(cs_end - cs_c)
        decay_chunk = jnp.exp(cs_end)
        if y_ref is not None:
            cs_row = jnp.sum(jnp.where(diag, cs_c, 0.0), axis=0, keepdims=True)
            decay_l = jnp.where(causal, jnp.exp(cs_c - cs_row), 0.0)
            decay_in = jnp.exp(cs_c)
        for g in range(SSD_GROUPS):
            gc = slice(g * GROUP_COLS, (g + 1) * GROUP_COLS)
            gn = slice(g * SSD_STATE, (g + 1) * SSD_STATE)
            bm_g = bm[sl, gn]
            x_g = xdt[sl, gc]
            st = state_ref[g]
            if y_ref is not None:
                cm_g = cm[sl, gn]
                bm_rep = jnp.concatenate([bm_g] * HEADS_PER_GROUP, axis=0)
                scores = lax.dot_general(cm_g, bm_rep, (((1,), (1,)), ((), ())),
                                         preferred_element_type=F32)
                m = (scores * decay_l[:, gc]).astype(BF16)
                x_rep = jnp.concatenate([x_g] * HEADS_PER_GROUP, axis=0)
                x_bd = jnp.where(same_head, x_rep, 0.0).astype(BF16)
                y_diag = jnp.dot(m, x_bd, preferred_element_type=F32)
                y_off = jnp.dot(cm_g, st.astype(BF16), preferred_element_type=F32) * decay_in[:, gc]
                y_ref[sl, gc] = y_diag + y_off
            xd = (x_g * decay_out[:, gc]).astype(BF16)
            upd = lax.dot_general(bm_g, xd, (((0,), (0,)), ((), ())), preferred_element_type=F32)
            state_ref[g] = st * decay_chunk[:, gc] + upd
    return xs


def _ssd_kernel(xbc_ref, zs_ref, dt_ref, xbc_head_ref, dt_head_ref, cw_ref, cb_ref, alog_ref,
                dskip_ref, nw_ref, hexp_ref, o_ref, ext_ref, state_ref, y_ref, *, rows):
    refs = (cw_ref, cb_ref, alog_ref, hexp_ref, ext_ref, state_ref)

    @pl.when(pl.program_id(1) == 0)
    def _():
        ext_ref[0:SUBLANES, :] = jnp.zeros((SUBLANES, SSD_XBC), F32)
        state_ref[...] = jnp.zeros(state_ref.shape, F32)
        meta_row = lax.broadcasted_iota(jnp.int32, (CHUNK, 1), 0) >= PAD_ROWS
        _ssd_block(jnp.where(meta_row, xbc_head_ref[...].astype(F32), 0.0),
                   jnp.where(meta_row, dt_head_ref[...], 0.0), CHUNK, refs, None)

    xs = _ssd_block(xbc_ref[...].astype(F32), dt_ref[...], rows, refs, y_ref)
    y = y_ref[...] + dskip_ref[...] * xs
    yf = y * zs_ref[...].astype(F32)
    for g in range(SSD_GROUPS):
        gc = slice(g * GROUP_COLS, (g + 1) * GROUP_COLS)
        seg = yf[:, gc]
        ms = jnp.mean(seg * seg, axis=-1, keepdims=True)
        o_ref[:, gc] = (seg * lax.rsqrt(ms + LN_EPS) * nw_ref[:, gc]).astype(o_ref.dtype)


def _ssd_call(xbc, zs, dt, cw, cb, alog, dskip, nw, hexp, bsz, seq, rows):
    nblk = seq // rows
    head0 = bsz * seq // CHUNK
    rmap = lambda b, i: (b * nblk + i, 0)
    hmap = lambda b, i: (head0 + b, 0)
    cmap = lambda b, i: (0, 0)
    return pl.pallas_call(
        functools.partial(_ssd_kernel, rows=rows),
        grid=(bsz, nblk),
        in_specs=[pl.BlockSpec((rows, SSD_XBC), rmap),
                  pl.BlockSpec((rows, SSD_D_INNER), rmap),
                  pl.BlockSpec((rows, LANES), rmap),
                  pl.BlockSpec((CHUNK, SSD_XBC), hmap),
                  pl.BlockSpec((CHUNK, LANES), hmap),
                  pl.BlockSpec((SSD_CONV, SSD_XBC), cmap),
                  pl.BlockSpec((1, SSD_XBC), cmap),
                  pl.BlockSpec((1, LANES), cmap),
                  pl.BlockSpec((1, SSD_D_INNER), cmap),
                  pl.BlockSpec((1, SSD_D_INNER), cmap),
                  pl.BlockSpec((3 * LANES, SSD_D_INNER), cmap)],
        out_specs=pl.BlockSpec((rows, SSD_D_INNER), rmap),
        out_shape=jax.ShapeDtypeStruct((bsz * seq, SSD_D_INNER), BF16),
        scratch_shapes=[pltpu.VMEM((SUBLANES + rows, SSD_XBC), F32),
                        pltpu.VMEM((SSD_GROUPS, SSD_STATE, GROUP_COLS), F32),
                        pltpu.VMEM((rows, SSD_D_INNER), F32)],
        compiler_params=_cparams(("arbitrary", "arbitrary")),
        name="ssd",
    )(xbc, zs, dt, xbc, dt, cw, cb, alog, dskip, nw, hexp)


CF_ROW_BLK = 64
CF_COL_BLK = 256


def _glu_proj(x, wv_ref, wg_ref, cols):
    val = jnp.dot(x, wv_ref[:, cols], preferred_element_type=F32)
    gate = jnp.dot(x, wg_ref[:, cols], preferred_element_type=F32)
    return val * jax.nn.sigmoid(gate)


def _conf_kernel(x0_ref, xnext_ref, xhead_ref, wv_ref, wg_ref, w_ref, b_ref, g_ref, beta_ref, o_ref,
                 ext_ref, sh_ref, acc_ref, cnext_ref, *, rows):
    col_blocks = CF_WIDTH // CF_COL_BLK

    @pl.when(pl.program_id(1) == 0)
    def _():
        meta_row = lax.broadcasted_iota(jnp.int32, (CF_HALO, 1), 0) >= CF_HALO - N_META
        x_tail = xhead_ref[CHUNK - CF_HALO:CHUNK, :]
        for cb in range(col_blocks):
            cols = slice(cb * CF_COL_BLK, (cb + 1) * CF_COL_BLK)
            ext_ref[0:CF_HALO, cols] = jnp.where(meta_row, _glu_proj(x_tail, wv_ref, wg_ref, cols), 0.0)
            cnext_ref[:, cols] = _glu_proj(x0_ref[...], wv_ref, wg_ref, cols)

    ext_ref[CF_HALO:CF_HALO + rows, :] = cnext_ref[...]
    span = rows + CF_HALO - SUBLANES
    for s in range(1, SUBLANES):
        sh_ref[s - 1] = ext_ref[pl.ds(s, span), :]
    base = CF_HALO - (CF_CONV - 1)

    def col_block(cb, carry):
        cols = pl.ds(pl.multiple_of(cb * CF_COL_BLK, CF_COL_BLK), CF_COL_BLK)
        nxt = _glu_proj(xnext_ref[...], wv_ref, wg_ref, cols)
        for rb in range(rows // CF_ROW_BLK):
            r0 = rb * CF_ROW_BLK
            acc = jnp.broadcast_to(b_ref[:, cols], (CF_ROW_BLK, CF_COL_BLK))
            for k in range(CF_CONV):
                s = (base + k) % SUBLANES
                q = r0 + (base + k - s)
                if s == 0:
                    src = ext_ref[q:q + CF_ROW_BLK, cols]
                else:
                    src = sh_ref[s - 1, q:q + CF_ROW_BLK, cols]
                acc = acc + w_ref[k:k + 1, cols] * src
            acc_ref[r0:r0 + CF_ROW_BLK, cols] = acc
        cnext_ref[:, cols] = nxt
        return carry

    lax.fori_loop(0, col_blocks, col_block, 0)
    ext_ref[0:CF_HALO, :] = ext_ref[rows:rows + CF_HALO, :]
    o_ref[...] = _silu(_layer_norm(acc_ref[...], g_ref[...], beta_ref[...])).astype(o_ref.dtype)


def _conf_call(hb, wv, wg, w, b, g, beta, bsz, seq, rows):
    nblk = seq // rows
    head0 = bsz * seq // CHUNK
    rmap = lambda bb, i: (bb * nblk + i, 0)
    first = lambda bb, i: (bb * nblk, 0)
    nxt = lambda bb, i: (bb * nblk + jnp.minimum(i + 1, nblk - 1), 0)
    hmap = lambda bb, i: (head0 + bb, 0)
    cmap = lambda bb, i: (0, 0)
    return pl.pallas_call(
        functools.partial(_conf_kernel, rows=rows),
        grid=(bsz, nblk),
        in_specs=[pl.BlockSpec((rows, D_MODEL), first),
                  pl.BlockSpec((rows, D_MODEL), nxt),
                  pl.BlockSpec((CHUNK, D_MODEL), hmap),
                  _resident((D_MODEL, CF_WIDTH)),
                  _resident((D_MODEL, CF_WIDTH)),
                  pl.BlockSpec((CF_CONV, CF_WIDTH), cmap),
                  pl.BlockSpec((1, CF_WIDTH), cmap),
                  pl.BlockSpec((1, CF_WIDTH), cmap),
                  pl.BlockSpec((1, CF_WIDTH), cmap)],
        out_specs=pl.BlockSpec((rows, CF_WIDTH), rmap),
        out_shape=jax.ShapeDtypeStruct((bsz * seq, CF_WIDTH), BF16),
        scratch_shapes=[pltpu.VMEM((CF_HALO + rows, CF_WIDTH), F32),
                        pltpu.VMEM((SUBLANES - 1, CF_HALO + rows - SUBLANES, CF_WIDTH), F32),
                        pltpu.VMEM((rows, CF_WIDTH), F32),
                        pltpu.VMEM((rows, CF_WIDTH), F32)],
        compiler_params=_cparams(("arbitrary", "arbitrary")),
        name="conformer",
    )(hb, hb, hb, wv, wg, w, b, g, beta)


def _merge_kernel(yn_ref, c_ref, gate_ref, h_ref, wssd_ref, wcf_ref, wo_ref, g_ref, b_ref,
                  o_ref, ob_ref):
    y_ssd = jnp.dot(yn_ref[...], wssd_ref[...], preferred_element_type=F32)
    y_cf = jnp.dot(c_ref[...], wcf_ref[...], preferred_element_type=F32)
    gates = gate_ref[...].astype(F32)
    mix = gates[:, :D_MODEL] * y_ssd + gates[:, D_MODEL:] * y_cf
    m = jnp.dot(mix.astype(BF16), wo_ref[...], preferred_element_type=F32)
    h1 = _layer_norm(DN_ALPHA * h_ref[...] + m, g_ref[...], b_ref[...])
    o_ref[...] = h1
    ob_ref[...] = h1.astype(BF16)


def _merge_call(yn, c, gates, h0, wssd, wcf, wo, g, b, tm):
    t = yn.shape[0]
    rmap = lambda i: (i, 0)
    return pl.pallas_call(
        _merge_kernel,
        grid=(t // tm,),
        in_specs=[pl.BlockSpec((tm, SSD_D_INNER), rmap),
                  pl.BlockSpec((tm, CF_WIDTH), rmap),
                  pl.BlockSpec((tm, 2 * D_MODEL), rmap),
                  pl.BlockSpec((tm, D_MODEL), rmap),
                  _resident((SSD_D_INNER, D_MODEL)),
                  _resident((CF_WIDTH, D_MODEL)),
                  _resident((D_MODEL, D_MODEL)),
                  _resident((1, D_MODEL)),
                  _resident((1, D_MODEL))],
        out_specs=[pl.BlockSpec((tm, D_MODEL), rmap), pl.BlockSpec((tm, D_MODEL), rmap)],
        out_shape=[jax.ShapeDtypeStruct((t, D_MODEL), F32),
                   jax.ShapeDtypeStruct((t, D_MODEL), BF16)],
        compiler_params=_cparams(("parallel",)),
        name="merge",
    )(yn, c, gates, h0, wssd, wcf, wo, g, b)


def _slab(ref, j):
    if isinstance(j, int):
        return ref.at[:, j * PEER_HEADS:(j + 1) * PEER_HEADS, :]
    return ref.at[:, pl.ds(pl.multiple_of(j * PEER_HEADS, PEER_HEADS), PEER_HEADS), :]


def _top16_major(s_ref, ix_ref, val_ref, idx_ref, n):
    neg = jnp.float32(-jnp.inf)
    half = n // 2
    m0 = None
    for p in range(half):
        a = _slab(s_ref, p)[...]
        b = _slab(s_ref, p + half)[...]
        swap = b > a
        front = jnp.where(swap, b, a)
        _slab(s_ref, p)[...] = front
        _slab(s_ref, p + half)[...] = jnp.where(swap, a, b)
        _slab(ix_ref, p)[...] = jnp.where(swap, p + half, p)
        _slab(ix_ref, p + half)[...] = jnp.where(swap, p, p + half)
        m0 = front if m0 is None else jnp.maximum(m0, front)

    def body(it, m):
        sel = jnp.full(m.shape, n, jnp.int32)
        for p in range(half):
            sel = jnp.minimum(sel, jnp.where(_slab(s_ref, p)[...] == m, _slab(ix_ref, p)[...], n))
        _slab(val_ref, it)[...] = m
        _slab(idx_ref, it)[...] = sel
        nxt = jnp.full(m.shape, neg, F32)
        for p in range(half):
            hit = _slab(ix_ref, p)[...] == sel
            back = _slab(s_ref, p + half)[...]
            front = jnp.where(hit, back, _slab(s_ref, p)[...])
            _slab(s_ref, p)[...] = front
            _slab(s_ref, p + half)[...] = jnp.where(hit, neg, back)
            _slab(ix_ref, p)[...] = jnp.where(hit, _slab(ix_ref, p + half)[...], _slab(ix_ref, p)[...])
            nxt = jnp.maximum(nxt, front)
        return nxt

    lax.fori_loop(0, PEER_TOPK, body, m0)


def _peer_query_kernel(hb_ref, wq_ref, keys_ref, i1_ref, i2_ref, gate_ref,
                       s_ref, ix_ref, v1_ref, x1_ref, v2_ref, x2_ref, cand_ref, bs_ref, e1_ref, e2_ref):
    tq = hb_ref.shape[0]
    lane_blocks = tq // LANES
    q = jnp.dot(hb_ref[...], wq_ref[...], preferred_element_type=F32).astype(BF16)
    half_cols = PEER_DKEY // 2
    for half, (v_ref, x_ref) in enumerate(((v1_ref, x1_ref), (v2_ref, x2_ref))):
        for h in range(PEER_HEADS):
            c0 = h * PEER_DKEY + half * half_cols
            scores = lax.dot_general(keys_ref[half, h], q[:, c0:c0 + half_cols],
                                     (((1,), (1,)), ((), ())), preferred_element_type=F32)
            for lb in range(lane_blocks):
                s_ref[lb, pl.ds(h, PEER_NKEYS, stride=PEER_HEADS), :] = (
                    scores[:, lb * LANES:(lb + 1) * LANES])
        _top16_major(s_ref, ix_ref, v_ref, x_ref, PEER_NKEYS)

    for ci, (a, b) in enumerate(_CAND):
        _slab(cand_ref, ci)[...] = _slab(v1_ref, a)[...] + _slab(v2_ref, b)[...]
    neg = jnp.float32(-jnp.inf)
    m0 = _slab(cand_ref, 0)[...]
    for ci in range(1, len(_CAND)):
        m0 = jnp.maximum(m0, _slab(cand_ref, ci)[...])
    big = PEER_TOPK * PEER_TOPK

    def body(it, m):
        sel = jnp.full(m.shape, big, jnp.int32)
        for ci, (a, b) in enumerate(_CAND):
            sel = jnp.minimum(sel, jnp.where(_slab(cand_ref, ci)[...] == m, a * PEER_TOPK + b, big))
        e1 = jnp.zeros(m.shape, jnp.int32)
        e2 = jnp.zeros(m.shape, jnp.int32)
        nxt = jnp.full(m.shape, neg, F32)
        for ci, (a, b) in enumerate(_CAND):
            hit = sel == a * PEER_TOPK + b
            e1 = jnp.where(hit, _slab(x1_ref, a)[...], e1)
            e2 = jnp.where(hit, _slab(x2_ref, b)[...], e2)
            cj = jnp.where(hit, neg, _slab(cand_ref, ci)[...])
            _slab(cand_ref, ci)[...] = cj
            nxt = jnp.maximum(nxt, cj)
        _slab(bs_ref, it)[...] = m
        _slab(e1_ref, it)[...] = e1
        _slab(e2_ref, it)[...] = e2
        return nxt

    lax.fori_loop(0, PEER_TOPK, body, m0)

    for lb in range(lane_blocks):
        bs = bs_ref[lb].reshape(PEER_TOPK, PEER_HEADS, LANES)
        ex = jnp.exp(bs - bs[0:1])
        gate = ex / jnp.sum(ex, axis=0, keepdims=True)
        tok = slice(lb * LANES, (lb + 1) * LANES)
        gate_ref[tok, :] = gate.reshape(PEER_TOPK * PEER_HEADS, LANES).T
        i1_ref[tok, :] = e1_ref[lb].T
        i2_ref[tok, :] = e2_ref[lb].T


def _peer_query_call(hb, wq, keys, tq):
    t = hb.shape[0]
    nj = PEER_TOPK * PEER_HEADS
    rmap = lambda i: (i, 0)
    slabs = lambda n, dt: pltpu.VMEM((tq // LANES, n * PEER_HEADS, LANES), dt)
    return pl.pallas_call(
        _peer_query_kernel,
        grid=(t // tq,),
        in_specs=[pl.BlockSpec((tq, D_MODEL), rmap),
                  _resident((D_MODEL, PEER_HEADS * PEER_DKEY)),
                  _resident((2, PEER_HEADS, PEER_NKEYS, PEER_DKEY // 2))],
        out_specs=[pl.BlockSpec((tq, nj), rmap)] * 3,
        out_shape=[jax.ShapeDtypeStruct((t, nj), jnp.int32),
                   jax.ShapeDtypeStruct((t, nj), jnp.int32),
                   jax.ShapeDtypeStruct((t, nj), F32)],
        scratch_shapes=[slabs(PEER_NKEYS, F32), slabs(PEER_NKEYS, jnp.int32),
                        slabs(PEER_TOPK, F32), slabs(PEER_TOPK, jnp.int32),
                        slabs(PEER_TOPK, F32), slabs(PEER_TOPK, jnp.int32),
                        slabs(len(_CAND), F32), slabs(PEER_TOPK, F32),
                        slabs(PEER_TOPK, jnp.int32), slabs(PEER_TOPK, jnp.int32)],
        compiler_params=_cparams(("parallel",)),
        name="peer_query",
    )(hb, wq, keys)


BUILD_UNROLL = 16


def _gelu(x):
    return 0.5 * x * (1.0 + lax.erf(x * (1.0 / math.sqrt(2.0))))


def _pack_bf16_pair(a, b):
    bits = lambda v: lax.bitcast_convert_type(v.astype(BF16).astype(F32), jnp.uint32)
    return bits(a) | (bits(b) >> 16)


def _unpack_bf16_pair(w):
    return (lax.bitcast_convert_type(w & jnp.uint32(0xFFFF0000), F32),
            lax.bitcast_convert_type(w << 16, F32))


def _route_rows(i1_row, i2_row, gate_row, key_iota):
    a_t = jnp.where(key_iota == i1_row, gate_row, 0.0).astype(BF16)
    b_t = jnp.where(key_iota == i2_row, 1.0, 0.0).astype(BF16)
    return lax.dot_general(a_t, b_t, (((1,), (1,)), ((), ())), preferred_element_type=F32)


def _peer_expert_kernel(hb_ref, h_ref, i1_ref, i2_ref, gate_ref, u_ref, v_ref, g_ref, b_ref,
                        o_ref, gs_ref, *, tm, te, stride):
    e = pl.program_id(1)
    half = tm // 2

    @pl.when(e == 0)
    def _():
        o_ref[...] = jnp.zeros(o_ref.shape, F32)
        key_iota = lax.broadcasted_iota(jnp.int32, (PEER_NKEYS, PEER_NKEYS), 0)

        def build(blk, carry):
            lo = pl.multiple_of(blk * BUILD_UNROLL, BUILD_UNROLL)
            hi = pl.multiple_of(half + blk * BUILD_UNROLL, BUILD_UNROLL)
            rows = [(r[pl.ds(lo, BUILD_UNROLL), :], r[pl.ds(hi, BUILD_UNROLL), :])
                    for r in (i1_ref, i2_ref, gate_ref)]
            for k in range(BUILD_UNROLL):
                g0 = _route_rows(rows[0][0][k:k + 1], rows[1][0][k:k + 1], rows[2][0][k:k + 1], key_iota)
                g1 = _route_rows(rows[0][1][k:k + 1], rows[1][1][k:k + 1], rows[2][1][k:k + 1], key_iota)
                gs_ref[pl.ds(lo + k, PEER_NKEYS, stride=stride), :] = _pack_bf16_pair(g0, g1)
            return carry

        lax.fori_loop(0, half // BUILD_UNROLL, build, 0)

    s = lax.dot_general(hb_ref[...], u_ref[...], (((1,), (1,)), ((), ())),
                        preferred_element_type=F32)
    slabs = te // PEER_NKEYS
    packed = jnp.concatenate(
        [gs_ref[pl.ds(pl.multiple_of((e * slabs + r) * stride, SUBLANES), half), :]
         for r in range(slabs)], axis=1)
    route = jnp.concatenate(_unpack_bf16_pair(packed), axis=0)
    act = (_gelu(s) * route).astype(BF16)
    o_ref[...] += jnp.dot(act, v_ref[...], preferred_element_type=F32)

    @pl.when(e == pl.num_programs(1) - 1)
    def _():
        o_ref[...] = _layer_norm(DN_ALPHA * h_ref[...] + o_ref[...], g_ref[...], b_ref[...])


def _peer_expert_call(hb, h, i1, i2, gate, u, v, g, b, tm, te):
    t = hb.shape[0]
    nj = PEER_TOPK * PEER_HEADS
    assert tm % (2 * BUILD_UNROLL) == 0
    stride = tm // 2 + SUBLANES
    rmap = lambda i, e: (i, 0)
    emap = lambda i, e: (e, 0)
    cmap = lambda i, e: (0, 0)
    once = dict(pipeline_mode=pl.Buffered(1))
    return pl.pallas_call(
        functools.partial(_peer_expert_kernel, tm=tm, te=te, stride=stride),
        grid=(t // tm, PEER_EXPERTS // te),
        in_specs=[pl.BlockSpec((tm, D_MODEL), rmap, **once),
                  pl.BlockSpec((tm, D_MODEL), rmap, **once),
                  pl.BlockSpec((tm, nj), rmap, **once),
                  pl.BlockSpec((tm, nj), rmap, **once),
                  pl.BlockSpec((tm, nj), rmap, **once),
                  pl.BlockSpec((te, D_MODEL), emap),
                  pl.BlockSpec((te, D_MODEL), emap),
                  pl.BlockSpec((1, D_MODEL), cmap),
                  pl.BlockSpec((1, D_MODEL), cmap)],
        out_specs=pl.BlockSpec((tm, D_MODEL), rmap),
        out_shape=jax.ShapeDtypeStruct((t, D_MODEL), F32),
        scratch_shapes=[pltpu.VMEM((PEER_NKEYS * stride, PEER_NKEYS), jnp.uint32)],
        compiler_params=_cparams(("parallel", "arbitrary"), PEER_VMEM_LIMIT),
        name="peer_experts",
    )(hb, h, i1, i2, gate, u, v, g, b)


def kernel(x, meta, ln0_g, ln0_b, w_in, ssd_conv_w, ssd_conv_b, ssd_dt_bias, ssd_a_log, ssd_d,
           ssd_norm_w, ssd_out, cf_dw_w, cf_dw_b, cf_ln_g, cf_ln_b, cf_out, w_o, ln1_g, ln1_b,
           peer_wq, peer_keys, peer_u, peer_v, ln2_g, ln2_b):
    bsz, seq, d = x.shape
    assert d == D_MODEL and seq % CHUNK == 0
    assert w_in.shape[0] == 1, "single layer"
    t = bsz * seq
    tp = t + bsz * CHUNK
    ln_rows = _divisor_tile(t, 512, 2 * CHUNK)
    assert bsz * CHUNK <= ln_rows

    rows = _divisor_tile(seq, 256, CHUNK)
    tm_proj = _divisor_tile(tp, 1664, SUBLANES)
    tm_merge = _divisor_tile(t, 256, SUBLANES)
    tq = _divisor_tile(t, 512, LANES)
    tm_peer = _divisor_tile(t, 512, 2 * BUILD_UNROLL)

    h0, h0b = _ln0_call(x.reshape(t, d), meta, ln0_g, ln0_b, tp, ln_rows)

    w = w_in[0]
    wz = w[:, :OFF_Z].astype(BF16)
    wxbc = w[:, OFF_Z:OFF_XBC].astype(BF16)
    wdt = jnp.pad(w[:, OFF_XBC:OFF_DT], ((0, 0), (0, LANES - SSD_HEADS))).astype(BF16)
    wcv =w[:, OFF_DT:OFF_DT + CF_WIDTH].astype(BF16)
    wcg = w[:, OFF_DT + CF_WIDTH:OFF_CF].astype(BF16)
    wgate = w[:, OFF_CF:].astype(BF16)
    pad_heads = lambda a: jnp.pad(a.astype(F32), (0, LANES - SSD_HEADS)).reshape(1, LANES)
    head_expand = jnp.tile(jnp.arange(SSD_D_INNER)[None, :] // SSD_HEADDIM
                           == jnp.arange(LANES)[:, None], (3, 1)).astype(BF16)

    zs = _proj_call(h0b, [wz], _silu, BF16, tm_proj, 512, name="proj_z")
    xbc = _proj_call(h0b, [wxbc], lambda a: a, BF16, tm_proj, 512, name="proj_xbc")
    dt = _proj_call(h0b, [wdt], _softplus, F32, tm_proj, LANES, bias=pad_heads(ssd_dt_bias[0]),
                    name="proj_dt")
    gates = _proj_call(h0b, [wgate], jax.nn.sigmoid, BF16, tm_proj, 512, name="proj_gate")

    yn = _ssd_call(xbc, zs, dt, ssd_conv_w[0], ssd_conv_b[0].reshape(1, -1), pad_heads(ssd_a_log[0]),
                   jnp.repeat(ssd_d[0].astype(F32), SSD_HEADDIM).reshape(1, -1),
                   ssd_norm_w[0].reshape(1, -1), head_expand, bsz, seq, rows)
    c2 = _conf_call(h0b, wcv, wcg, cf_dw_w[0], cf_dw_b[0].reshape(1, -1), cf_ln_g[0].reshape(1, -1),
                    cf_ln_b[0].reshape(1, -1), bsz, seq, rows)
    h1, h1b = _merge_call(yn, c2, gates, h0, ssd_out[0].astype(BF16), cf_out[0].astype(BF16),
                          w_o[0].astype(BF16), ln1_g[0].reshape(1, -1), ln1_b[0].reshape(1, -1),
                          tm_merge)

    i1, i2, gate = _peer_query_call(h1b, peer_wq[0].astype(BF16), peer_keys[0].astype(BF16), tq)
    out = _peer_expert_call(h1b, h1, i1, i2, gate, peer_u[0].astype(BF16), peer_v[0].astype(BF16),
                            ln2_g[0].reshape(1, -1), ln2_b[0].reshape(1, -1), tm_peer, 1024)
    return out.reshape(bsz, seq, d)
```

```python
import functools
import math

import jax
import jax.numpy as jnp
from jax import lax
from jax.experimental import pallas as pl
from jax.experimental.pallas import tpu as pltpu

F32 = jnp.float32
BF16 = jnp.bfloat16

D_MODEL = 2048
CHUNK = 64
N_META = 16
PAD_ROWS = CHUNK - N_META
SSD_D_INNER = 2048
SSD_HEADDIM = 64
SSD_HEADS = 32
SSD_GROUPS = 8
SSD_STATE = 128
SSD_CONV = 4
SSD_XBC = SSD_D_INNER + 2 * SSD_GROUPS * SSD_STATE
GROUP_COLS = SSD_D_INNER // SSD_GROUPS
HEADS_PER_GROUP = SSD_HEADS // SSD_GROUPS
CF_WIDTH = 2048
CF_CONV = 31
CF_HALO = 32
PEER_HEADS = 8
PEER_NKEYS = 128
PEER_EXPERTS = PEER_NKEYS * PEER_NKEYS
PEER_DKEY = 256
PEER_TOPK = 16
OFF_Z = SSD_D_INNER
OFF_XBC = OFF_Z + SSD_XBC
OFF_DT = OFF_XBC + SSD_HEADS
OFF_CF = OFF_DT + 2 * CF_WIDTH
LN_EPS = 1e-5
DN_ALPHA = 2.0 ** 0.25
LANES = 128
SUBLANES = 8
VMEM_LIMIT = 56 * 1024 * 1024
PEER_VMEM_LIMIT = 60 * 1024 * 1024

_CAND = [(a, b) for a in range(PEER_TOPK) for b in range(PEER_TOPK)
         if (a + 1) * (b + 1) <= PEER_TOPK]


def _cparams(sem, vmem_limit=VMEM_LIMIT):
    return pltpu.CompilerParams(dimension_semantics=sem, vmem_limit_bytes=vmem_limit)


def _resident(shape):
    nd = len(shape)
    return pl.BlockSpec(shape, lambda *_: (0,) * nd, pipeline_mode=pl.Buffered(1))


def _divisor_tile(n, target, mult):
    best = None
    for t in range(mult, min(n, target) + 1, mult):
        if n % t == 0:
            best = t
    assert best is not None, (n, target, mult)
    return best


def _layer_norm(x, g, b):
    mu = jnp.mean(x, axis=-1, keepdims=True)
    xc = x - mu
    var = jnp.mean(xc * xc, axis=-1, keepdims=True)
    return xc * lax.rsqrt(var + LN_EPS) * g + b


def _silu(x):
    return x * jax.nn.sigmoid(x)


def _ln0_kernel(x_ref, meta_ref, g_ref, b_ref, o_ref, ob_ref, *, n_token_tiles):
    i = pl.program_id(0)

    def emit(rows):
        y = _layer_norm(rows, g_ref[...], b_ref[...])
        o_ref[...] = y
        ob_ref[...] = y.astype(BF16)

    @pl.when(i < n_token_tiles)
    def _():
        emit(x_ref[...])

    @pl.when(i >= n_token_tiles)
    def _():
        head = jnp.concatenate([jnp.zeros((PAD_ROWS, D_MODEL), F32), meta_ref[...]], axis=0)
        emit(jnp.concatenate([head] * (x_ref.shape[0] // CHUNK), axis=0))


def _ln0_call(x2d, meta, g, b, tp, rows):
    t, d = x2d.shape
    n_tok = t // rows
    return pl.pallas_call(
        functools.partial(_ln0_kernel, n_token_tiles=n_tok),
        grid=(pl.cdiv(tp, rows),),
        in_specs=[pl.BlockSpec((rows, d), lambda i: (jnp.minimum(i, n_tok - 1), 0)),
                  pl.BlockSpec((N_META, d), lambda i: (0, 0)),
                  pl.BlockSpec((1, d), lambda i: (0, 0)),
                  pl.BlockSpec((1, d), lambda i: (0, 0))],
        out_specs=[pl.BlockSpec((rows, d), lambda i: (i, 0)),
                   pl.BlockSpec((rows, d), lambda i: (i, 0))],
        out_shape=[jax.ShapeDtypeStruct((tp, d), F32), jax.ShapeDtypeStruct((tp, d), BF16)],
        compiler_params=_cparams(("parallel",)),
        name="ln0",
    )(x2d, meta, g.reshape(1, d), b.reshape(1, d))


def _proj_kernel(x_ref, *refs, n_w, epilogue, has_bias):
    w_refs = refs[:n_w]
    b_ref = refs[n_w] if has_bias else None
    o_ref = refs[-1]
    x = x_ref[...]
    accs = [jnp.dot(x, w[...], preferred_element_type=F32) for w in w_refs]
    if has_bias:
        accs[0] = accs[0] + b_ref[...]
    o_ref[...] = epilogue(*accs).astype(o_ref.dtype)


def _proj_call(x, ws, epilogue, out_dtype, tm, tn, bias=None, name="proj"):
    tp, k = x.shape
    n = ws[0].shape[1]
    in_specs = [pl.BlockSpec((tm, k), lambda i, j: (i, 0))]
    in_specs += [pl.BlockSpec((k, tn), lambda i, j: (0, j)) for _ in ws]
    args = [x, *ws]
    if bias is not None:
        in_specs.append(pl.BlockSpec((1, tn), lambda i, j: (0, j)))
        args.append(bias)
    return pl.pallas_call(
        functools.partial(_proj_kernel, n_w=len(ws), epilogue=epilogue, has_bias=bias is not None),
        grid=(tp // tm, n // tn),
        in_specs=in_specs,
        out_specs=pl.BlockSpec((tm, tn), lambda i, j: (i, j)),
        out_shape=jax.ShapeDtypeStruct((tp, n), out_dtype),
        compiler_params=_cparams(("parallel", "parallel")),
        name=name,
    )(*args)


def _softplus(x):
    return jnp.maximum(x, 0.0) + jnp.log1p(jnp.exp(-jnp.abs(x)))


def _split3(x):
    hi = x.astype(BF16)
    r1 = x - hi.astype(F32)
    mid = r1.astype(BF16)
    lo = (r1 - mid.astype(F32)).astype(BF16)
    return jnp.concatenate([hi, mid, lo], axis=1)


def _ssd_block(xin, dt, n, refs, y_ref):
    cw_ref, cb_ref, alog_ref, hexp_ref, ext_ref, state_ref = refs
    hist = SUBLANES
    ext_ref[hist:hist + n, :] = xin
    acc = cb_ref[...] + cw_ref[0:1, :] * ext_ref[pl.ds(hist - (SSD_CONV - 1), n), :]
    for k in range(1, SSD_CONV):
        acc = acc + cw_ref[k:k + 1, :] * ext_ref[pl.ds(hist - (SSD_CONV - 1) + k, n), :]
    ext_ref[0:hist, :] = ext_ref[n:n + hist, :]
    xc = _silu(acc)
    xs = xc[:, :SSD_D_INNER]
    bm = xc[:, SSD_D_INNER:SSD_D_INNER + SSD_GROUPS * SSD_STATE].astype(BF16)
    cm = xc[:, SSD_D_INNER + SSD_GROUPS * SSD_STATE:].astype(BF16)

    da = dt * (-jnp.exp(alog_ref[...]))
    pos = lax.broadcasted_iota(jnp.int32, (n, 1), 0) % CHUNK
    cs = da
    shift = 1
    while shift < CHUNK:
        cs = cs + jnp.where(pos >= shift, pltpu.roll(cs, shift, axis=0), 0.0)
        shift *= 2
    hexp = hexp_ref[...]
    dt_e = jnp.dot(_split3(dt), hexp, preferred_element_type=F32)
    cs_e = jnp.dot(_split3(cs), hexp, preferred_element_type=F32)
    xdt = xs * dt_e

    sub = lax.broadcasted_iota(jnp.int32, (CHUNK, SSD_D_INNER), 0)
    lane_pos = lax.broadcasted_iota(jnp.int32, (CHUNK, SSD_D_INNER), 1) % SSD_HEADDIM
    diag = sub == lane_pos
    causal = sub >= lane_pos
    blk_r = lax.broadcasted_iota(jnp.int32, (GROUP_COLS, GROUP_COLS), 0) // SSD_HEADDIM
    blk_c = lax.broadcasted_iota(jnp.int32, (GROUP_COLS, GROUP_COLS), 1) // SSD_HEADDIM
    same_head = blk_r == blk_c

    for c in range(n // CHUNK):
        sl = slice(c * CHUNK, (c + 1) * CHUNK)
        cs_c = cs_e[sl]
        cs_end = cs_c[CHUNK - 1:CHUNK, :]
        decay_out = jnp.exp(cs_end - cs_c)
        decay_chunk = jnp.exp(cs_end)
        if y_ref is not None:
            cs_row = jnp.sum(jnp.where(diag, cs_c, 0.0), axis=0, keepdims=True)
            decay_l = jnp.where(causal, jnp.exp(cs_c - cs_row), 0.0)
            decay_in = jnp.exp(cs_c)
        for g in range(SSD_GROUPS):
            gc = slice(g * GROUP_COLS, (g + 1) * GROUP_COLS)
            gn = slice(g * SSD_STATE, (g + 1) * SSD_STATE)
            bm_g = bm[sl, gn]
            x_g = xdt[sl, gc]
            st = state_ref[g]
            if y_ref is not None:
                cm_g = cm[sl, gn]
                bm_rep = jnp.concatenate([bm_g] * HEADS_PER_GROUP, axis=0)
                scores = lax.dot_general(cm_g, bm_rep, (((1,), (1,)), ((), ())),
                                         preferred_element_type=F32)
                m = (scores * decay_l[:, gc]).astype(BF16)
                x_rep = jnp.concatenate([x_g] * HEADS_PER_GROUP, axis=0)
                x_bd = jnp.where(same_head, x_rep, 0.0).astype(BF16)
                y_diag = jnp.dot(m, x_bd, preferred_element_type=F32)
                y_off = jnp.dot(cm_g, st.astype(BF16), preferred_element_type=F32) * decay_in[:, gc]
                y_ref[sl, gc] = y_diag + y_off
            xd = (x_g * decay_out[:, gc]).astype(BF16)
            upd = lax.dot_general(bm_g, xd, (((0,), (0,)), ((), ())), preferred_element_type=F32)
            state_ref[g] = st * decay_chunk[:, gc] + upd
    return xs


def _ssd_kernel(xbc_ref, zs_ref, dt_ref, xbc_head_ref, dt_head_ref, cw_ref, cb_ref, alog_ref,
                dskip_ref, nw_ref, hexp_ref, o_ref, ext_ref, state_ref, y_ref, *, rows):
    refs = (cw_ref, cb_ref, alog_ref, hexp_ref, ext_ref, state_ref)

    @pl.when(pl.program_id(1) == 0)
    def _():
        ext_ref[0:SUBLANES, :] = jnp.zeros((SUBLANES, SSD_XBC), F32)
        state_ref[...] = jnp.zeros(state_ref.shape, F32)
        meta_row = lax.broadcasted_iota(jnp.int32, (CHUNK, 1), 0) >= PAD_ROWS
        _ssd_block(jnp.where(meta_row, xbc_head_ref[...].astype(F32), 0.0),
                   jnp.where(meta_row, dt_head_ref[...], 0.0), CHUNK, refs, None)

    xs = _ssd_block(xbc_ref[...].astype(F32), dt_ref[...], rows, refs, y_ref)
    y = y_ref[...] + dskip_ref[...] * xs
    yf = y * zs_ref[...].astype(F32)
    for g in range(SSD_GROUPS):
        gc = slice(g * GROUP_COLS, (g + 1) * GROUP_COLS)
        seg = yf[:, gc]
        ms = jnp.mean(seg * seg, axis=-1, keepdims=True)
        o_ref[:, gc] = (seg * lax.rsqrt(ms + LN_EPS) * nw_ref[:, gc]).astype(o_ref.dtype)


def _ssd_call(xbc, zs, dt, cw, cb, alog, dskip, nw, hexp, bsz, seq, rows):
    nblk = seq // rows
    head0 = bsz * seq // CHUNK
    rmap = lambda b, i: (b * nblk + i, 0)
    hmap = lambda b, i: (head0 + b, 0)
    cmap = lambda b, i: (0, 0)
    return pl.pallas_call(
        functools.partial(_ssd_kernel, rows=rows),
        grid=(bsz, nblk),
        in_specs=[pl.BlockSpec((rows, SSD_XBC), rmap),
                  pl.BlockSpec((rows, SSD_D_INNER), rmap),
                  pl.BlockSpec((rows, LANES), rmap),
                  pl.BlockSpec((CHUNK, SSD_XBC), hmap),
                  pl.BlockSpec((CHUNK, LANES), hmap),
                  pl.BlockSpec((SSD_CONV, SSD_XBC), cmap),
                  pl.BlockSpec((1, SSD_XBC), cmap),
                  pl.BlockSpec((1, LANES), cmap),
                  pl.BlockSpec((1, SSD_D_INNER), cmap),
                  pl.BlockSpec((1, SSD_D_INNER), cmap),
                  pl.BlockSpec((3 * LANES, SSD_D_INNER), cmap)],
        out_specs=pl.BlockSpec((rows, SSD_D_INNER), rmap),
        out_shape=jax.ShapeDtypeStruct((bsz * seq, SSD_D_INNER), BF16),
        scratch_shapes=[pltpu.VMEM((SUBLANES + rows, SSD_XBC), F32),
                        pltpu.VMEM((SSD_GROUPS, SSD_STATE, GROUP_COLS), F32),
                        pltpu.VMEM((rows, SSD_D_INNER), F32)],
        compiler_params=_cparams(("arbitrary", "arbitrary")),
        name="ssd",
    )(xbc, zs, dt, xbc, dt, cw, cb, alog, dskip, nw, hexp)


CF_ROW_BLK = 64
CF_COL_BLK = 256


def _glu_proj(x, wv_ref, wg_ref, cols):
    val = jnp.dot(x, wv_ref[:, cols], preferred_element_type=F32)
    gate = jnp.dot(x, wg_ref[:, cols], preferred_element_type=F32)
    return val * jax.nn.sigmoid(gate)


def _conf_kernel(x0_ref, xnext_ref, xhead_ref, wv_ref, wg_ref, w_ref, b_ref, g_ref, beta_ref, o_ref,
                 ext_ref, sh_ref, acc_ref, cnext_ref, *, rows):
    col_blocks = CF_WIDTH // CF_COL_BLK

    @pl.when(pl.program_id(1) == 0)
    def _():
        meta_row = lax.broadcasted_iota(jnp.int32, (CF_HALO, 1), 0) >= CF_HALO - N_META
        x_tail = xhead_ref[CHUNK - CF_HALO:CHUNK, :]
        for cb in range(col_blocks):
            cols = slice(cb * CF_COL_BLK, (cb + 1) * CF_COL_BLK)
            ext_ref[0:CF_HALO, cols] = jnp.where(meta_row, _glu_proj(x_tail, wv_ref, wg_ref, cols), 0.0)
            cnext_ref[:, cols] = _glu_proj(x0_ref[...], wv_ref, wg_ref, cols)

    ext_ref[CF_HALO:CF_HALO + rows, :] = cnext_ref[...]
    span = rows + CF_HALO - SUBLANES
    for s in range(1, SUBLANES):
        sh_ref[s - 1] = ext_ref[pl.ds(s, span), :]
    base = CF_HALO - (CF_CONV - 1)

    def col_block(cb, carry):
        cols = pl.ds(pl.multiple_of(cb * CF_COL_BLK, CF_COL_BLK), CF_COL_BLK)
        nxt = _glu_proj(xnext_ref[...], wv_ref, wg_ref, cols)
        for rb in range(rows // CF_ROW_BLK):
            r0 = rb * CF_ROW_BLK
            acc = jnp.broadcast_to(b_ref[:, cols], (CF_ROW_BLK, CF_COL_BLK))
            for k in range(CF_CONV):
                s = (base + k) % SUBLANES
                q = r0 + (base + k - s)
                if s == 0:
                    src = ext_ref[q:q + CF_ROW_BLK, cols]
                else:
                    src = sh_ref[s - 1, q:q + CF_ROW_BLK, cols]
                acc = acc + w_ref[k:k + 1, cols] * src
            acc_ref[r0:r0 + CF_ROW_BLK, cols] = acc
        cnext_ref[:, cols] = nxt
        return carry

    lax.fori_loop(0, col_blocks, col_block, 0)
    ext_ref[0:CF_HALO, :] = ext_ref[rows:rows + CF_HALO, :]
    o_ref[...] = _silu(_layer_norm(acc_ref[...], g_ref[...], beta_ref[...])).astype(o_ref.dtype)


def _conf_call(hb, wv, wg, w, b, g, beta, bsz, seq, rows):
    nblk = seq // rows
    head0 = bsz * seq // CHUNK
    rmap = lambda bb, i: (bb * nblk + i, 0)
    first = lambda bb, i: (bb * nblk, 0)
    nxt = lambda bb, i: (bb * nblk + jnp.minimum(i + 1, nblk - 1), 0)
    hmap = lambda bb, i: (head0 + bb, 0)
    cmap = lambda bb, i: (0, 0)
    return pl.pallas_call(
        functools.partial(_conf_kernel, rows=rows),
        grid=(bsz, nblk),
        in_specs=[pl.BlockSpec((rows, D_MODEL), first),
                  pl.BlockSpec((rows, D_MODEL), nxt),
                  pl.BlockSpec((CHUNK, D_MODEL), hmap),
                  _resident((D_MODEL, CF_WIDTH)),
                  _resident((D_MODEL, CF_WIDTH)),
                  pl.BlockSpec((CF_CONV, CF_WIDTH), cmap),
                  pl.BlockSpec((1, CF_WIDTH), cmap),
                  pl.BlockSpec((1, CF_WIDTH), cmap),
                  pl.BlockSpec((1, CF_WIDTH), cmap)],
        out_specs=pl.BlockSpec((rows, CF_WIDTH), rmap),
        out_shape=jax.ShapeDtypeStruct((bsz * seq, CF_WIDTH), BF16),
        scratch_shapes=[pltpu.VMEM((CF_HALO + rows, CF_WIDTH), F32),
                        pltpu.VMEM((SUBLANES - 1, CF_HALO + rows - SUBLANES, CF_WIDTH), F32),
                        pltpu.VMEM((rows, CF_WIDTH), F32),
                        pltpu.VMEM((rows, CF_WIDTH), F32)],
        compiler_params=_cparams(("arbitrary", "arbitrary")),
        name="conformer",
    )(hb, hb, hb, wv, wg, w, b, g, beta)


def _merge_kernel(yn_ref, c_ref, gate_ref, h_ref, wssd_ref, wcf_ref, wo_ref, g_ref, b_ref,
                  o_ref, ob_ref):
    y_ssd = jnp.dot(yn_ref[...], wssd_ref[...], preferred_element_type=F32)
    y_cf = jnp.dot(c_ref[...], wcf_ref[...], preferred_element_type=F32)
    gates = gate_ref[...].astype(F32)
    mix = gates[:, :D_MODEL] * y_ssd + gates[:, D_MODEL:] * y_cf
    m = jnp.dot(mix.astype(BF16), wo_ref[...], preferred_element_type=F32)
    h1 = _layer_norm(DN_ALPHA * h_ref[...] + m, g_ref[...], b_ref[...])
    o_ref[...] = h1
    ob_ref[...] = h1.astype(BF16)


def _merge_call(yn, c, gates, h0, wssd, wcf, wo, g, b, tm):
    t = yn.shape[0]
    rmap = lambda i: (i, 0)
    return pl.pallas_call(
        _merge_kernel,
        grid=(t // tm,),
        in_specs=[pl.BlockSpec((tm, SSD_D_INNER), rmap),
                  pl.BlockSpec((tm, CF_WIDTH), rmap),
                  pl.BlockSpec((tm, 2 * D_MODEL), rmap),
                  pl.BlockSpec((tm, D_MODEL), rmap),
                  _resident((SSD_D_INNER, D_MODEL)),
                  _resident((CF_WIDTH, D_MODEL)),
                  _resident((D_MODEL, D_MODEL)),
                  _resident((1, D_MODEL)),
                  _resident((1, D_MODEL))],
        out_specs=[pl.BlockSpec((tm, D_MODEL), rmap), pl.BlockSpec((tm, D_MODEL), rmap)],
        out_shape=[jax.ShapeDtypeStruct((t, D_MODEL), F32),
                   jax.ShapeDtypeStruct((t, D_MODEL), BF16)],
        compiler_params=_cparams(("parallel",)),
        name="merge",
    )(yn, c, gates, h0, wssd, wcf, wo, g, b)


def _slab(ref, j):
    if isinstance(j, int):
        return ref.at[:, j * PEER_HEADS:(j + 1) * PEER_HEADS, :]
    return ref.at[:, pl.ds(pl.multiple_of(j * PEER_HEADS, PEER_HEADS), PEER_HEADS), :]


def _top16_major(s_ref, ix_ref, val_ref, idx_ref, n, ids=None):
    ids = list(range(n)) if ids is None else ids
    assert n % 2 == 0 and all(x < y for x, y in zip(ids, ids[1:]))
    none = ids[-1] + 1
    neg = jnp.float32(-jnp.inf)
    half = n // 2
    m0 = None
    for p in range(half):
        a = _slab(s_ref, p)[...]
        b = _slab(s_ref, p + half)[...]
        swap = b > a
        front = jnp.where(swap, b, a)
        _slab(s_ref, p)[...] = front
        _slab(s_ref, p + half)[...] = jnp.where(swap, a, b)
        _slab(ix_ref, p)[...] = jnp.where(swap, ids[p + half], ids[p])
        _slab(ix_ref, p + half)[...] = jnp.where(swap, ids[p], ids[p + half])
        m0 = front if m0 is None else jnp.maximum(m0, front)

    def body(it, m):
        sel = jnp.full(m.shape, none, jnp.int32)
        for p in range(half):
            sel = jnp.minimum(sel, jnp.where(_slab(s_ref, p)[...] == m, _slab(ix_ref, p)[...], none))
        _slab(val_ref, it)[...] = m
        _slab(idx_ref, it)[...] = sel
        nxt = jnp.full(m.shape, neg, F32)
        for p in range(half):
            hit = _slab(ix_ref, p)[...] == sel
            back = _slab(s_ref, p + half)[...]
            front = jnp.where(hit, back, _slab(s_ref, p)[...])
            _slab(s_ref, p)[...] = front
            _slab(s_ref, p + half)[...] = jnp.where(hit, neg, back)
            _slab(ix_ref, p)[...] = jnp.where(hit, _slab(ix_ref, p + half)[...], _slab(ix_ref, p)[...])
            nxt = jnp.maximum(nxt, front)
        return nxt

    lax.fori_loop(0, PEER_TOPK, body, m0)


def _peer_query_kernel(hb_ref, wq_ref, keys_ref, i1_ref, i2_ref, gate_ref,
                       s_ref, ix_ref, v1_ref, x1_ref, v2_ref, x2_ref, cand_ref, cix_ref, bs_ref, sel_ref,
                       e1_ref, e2_ref):
    tq = hb_ref.shape[0]
    lane_blocks = tq // LANES
    q = jnp.dot(hb_ref[...], wq_ref[...], preferred_element_type=F32).astype(BF16)
    half_cols = PEER_DKEY // 2
    for half, (v_ref, x_ref) in enumerate(((v1_ref, x1_ref), (v2_ref, x2_ref))):
        for h in range(PEER_HEADS):
            c0 = h * PEER_DKEY + half * half_cols
            scores = lax.dot_general(keys_ref[half, h], q[:, c0:c0 + half_cols],
                                     (((1,), (1,)), ((), ())), preferred_element_type=F32)
            for lb in range(lane_blocks):
                s_ref[lb, pl.ds(h, PEER_NKEYS, stride=PEER_HEADS), :] = (
                    scores[:, lb * LANES:(lb + 1) * LANES])
        _top16_major(s_ref, ix_ref, v_ref, x_ref, PEER_NKEYS)

    for ci, (a, b) in enumerate(_CAND):
        _slab(cand_ref, ci)[...] = _slab(v1_ref, a)[...] + _slab(v2_ref, b)[...]
    _top16_major(cand_ref, cix_ref, bs_ref, sel_ref, len(_CAND),
                 ids=[a * PEER_TOPK + b for a, b in _CAND])

    def key_ids(it, carry):
        flat = _slab(sel_ref, it)[...]
        k1 = lax.shift_right_logical(flat, int(math.log2(PEER_TOPK)))
        k2 = flat & (PEER_TOPK - 1)
        e1 = jnp.zeros(flat.shape, jnp.int32)
        e2 = jnp.zeros(flat.shape, jnp.int32)
        for r in range(PEER_TOPK):
            e1 = jnp.where(k1 == r, _slab(x1_ref, r)[...], e1)
            e2 = jnp.where(k2 == r, _slab(x2_ref, r)[...], e2)
        _slab(e1_ref, it)[...] = e1
        _slab(e2_ref, it)[...] = e2
        return carry

    lax.fori_loop(0, PEER_TOPK, key_ids, 0)

    for lb in range(lane_blocks):
        bs = bs_ref[lb].reshape(PEER_TOPK, PEER_HEADS, LANES)
        ex = jnp.exp(bs - bs[0:1])
        gate = ex / jnp.sum(ex, axis=0, keepdims=True)
        tok = slice(lb * LANES, (lb + 1) * LANES)
        gate_ref[tok, :] = gate.reshape(PEER_TOPK * PEER_HEADS, LANES).T
        i1_ref[tok, :] = e1_ref[lb].T
        i2_ref[tok, :] = e2_ref[lb].T


def _peer_query_call(hb, wq, keys, tq):
    t = hb.shape[0]
    nj = PEER_TOPK * PEER_HEADS
    rmap = lambda i: (i, 0)
    slabs = lambda n, dt: pltpu.VMEM((tq // LANES, n * PEER_HEADS, LANES), dt)
    return pl.pallas_call(
        _peer_query_kernel,
        grid=(t // tq,),
        in_specs=[pl.BlockSpec((tq, D_MODEL), rmap),
                  _resident((D_MODEL, PEER_HEADS * PEER_DKEY)),
                  _resident((2, PEER_HEADS, PEER_NKEYS, PEER_DKEY // 2))],
        out_specs=[pl.BlockSpec((tq, nj), rmap)] * 3,
        out_shape=[jax.ShapeDtypeStruct((t, nj), jnp.int32),
                   jax.ShapeDtypeStruct((t, nj), jnp.int32),
                   jax.ShapeDtypeStruct((t, nj), F32)],
        scratch_shapes=[slabs(PEER_NKEYS, F32), slabs(PEER_NKEYS, jnp.int32),
                        slabs(PEER_TOPK, F32), slabs(PEER_TOPK, jnp.int32),
                        slabs(PEER_TOPK, F32), slabs(PEER_TOPK, jnp.int32),
                        slabs(len(_CAND), F32), slabs(len(_CAND), jnp.int32),
                        slabs(PEER_TOPK, F32), slabs(PEER_TOPK, jnp.int32),
                        slabs(PEER_TOPK, jnp.int32), slabs(PEER_TOPK, jnp.int32)],
        compiler_params=_cparams(("parallel",)),
        name="peer_query",
    )(hb, wq, keys)


BUILD_UNROLL = 16


def _gelu(x):
    return 0.5 * x * (1.0 + lax.erf(x * (1.0 / math.sqrt(2.0))))


def _pack_bf16_pair(a, b):
    bits = lambda v: lax.bitcast_convert_type(v.astype(BF16).astype(F32), jnp.uint32)
    return bits(a) | (bits(b) >> 16)


def _unpack_bf16_pair(w):
    return (lax.bitcast_convert_type(w & jnp.uint32(0xFFFF0000), F32),
            lax.bitcast_convert_type(w << 16, F32))


def _route_rows(i1_row, i2_row, gate_row, key_iota):
    a_t = jnp.where(key_iota == i1_row, gate_row, 0.0).astype(BF16)
    b_t = jnp.where(key_iota == i2_row, 1.0, 0.0).astype(BF16)
    return lax.dot_general(a_t, b_t, (((1,), (1,)), ((), ())), preferred_element_type=F32)


def _peer_expert_kernel(hb_ref, h_ref, i1_ref, i2_ref, gate_ref, u_ref, v_ref, g_ref, b_ref,
                        o_ref, gs_ref, *, tm, te, stride):
    e = pl.program_id(1)
    half = tm // 2

    @pl.when(e == 0)
    def _():
        o_ref[...] = jnp.zeros(o_ref.shape, F32)
        key_iota = lax.broadcasted_iota(jnp.int32, (PEER_NKEYS, PEER_NKEYS), 0)

        def build(blk, carry):
            lo = pl.multiple_of(blk * BUILD_UNROLL, BUILD_UNROLL)
            hi = pl.multiple_of(half + blk * BUILD_UNROLL, BUILD_UNROLL)
            rows = [(r[pl.ds(lo, BUILD_UNROLL), :], r[pl.ds(hi, BUILD_UNROLL), :])
                    for r in (i1_ref, i2_ref, gate_ref)]
            for k in range(BUILD_UNROLL):
                g0 = _route_rows(rows[0][0][k:k + 1], rows[1][0][k:k + 1], rows[2][0][k:k + 1], key_iota)
                g1 = _route_rows(rows[0][1][k:k + 1], rows[1][1][k:k + 1], rows[2][1][k:k + 1], key_iota)
                gs_ref[pl.ds(lo + k, PEER_NKEYS, stride=stride), :] = _pack_bf16_pair(g0, g1)
            return carry

        lax.fori_loop(0, half // BUILD_UNROLL, build, 0)

    s = lax.dot_general(hb_ref[...], u_ref[...], (((1,), (1,)), ((), ())),
                        preferred_element_type=F32)
    slabs = te // PEER_NKEYS
    packed = jnp.concatenate(
        [gs_ref[pl.ds(pl.multiple_of((e * slabs + r) * stride, SUBLANES), half), :]
         for r in range(slabs)], axis=1)
    route = jnp.concatenate(_unpack_bf16_pair(packed), axis=0)
    act = (_gelu(s) * route).astype(BF16)
    o_ref[...] += jnp.dot(act, v_ref[...], preferred_element_type=F32)

    @pl.when(e == pl.num_programs(1) - 1)
    def _():
        o_ref[...] = _layer_norm(DN_ALPHA * h_ref[...] + o_ref[...], g_ref[...], b_ref[...])


def _peer_expert_call(hb, h, i1, i2, gate, u, v, g, b, tm, te):
    t = hb.shape[0]
    nj = PEER_TOPK * PEER_HEADS
    assert tm % (2 * BUILD_UNROLL) == 0
    stride = tm // 2 + SUBLANES
    rmap = lambda i, e: (i, 0)
    emap = lambda i, e: (e, 0)
    cmap = lambda i, e: (0, 0)
    once = dict(pipeline_mode=pl.Buffered(1))
    return pl.pallas_call(
        functools.partial(_peer_expert_kernel, tm=tm, te=te, stride=stride),
        grid=(t // tm, PEER_EXPERTS // te),
        in_specs=[pl.BlockSpec((tm, D_MODEL), rmap),
                  pl.BlockSpec((tm, D_MODEL), rmap, **once),
                  pl.BlockSpec((tm, nj), rmap),
                  pl.BlockSpec((tm, nj), rmap),
                  pl.BlockSpec((tm, nj), rmap),
                  pl.BlockSpec((te, D_MODEL), emap),
                  pl.BlockSpec((te, D_MODEL), emap),
                  pl.BlockSpec((1, D_MODEL), cmap),
                  pl.BlockSpec((1, D_MODEL), cmap)],
        out_specs=pl.BlockSpec((tm, D_MODEL), rmap),
        out_shape=jax.ShapeDtypeStruct((t, D_MODEL), F32),
        scratch_shapes=[pltpu.VMEM((PEER_NKEYS * stride, PEER_NKEYS), jnp.uint32)],
        compiler_params=_cparams(("parallel", "arbitrary"), PEER_VMEM_LIMIT),
        name="peer_experts",
    )(hb, h, i1, i2, gate, u, v, g, b)


def kernel(x, meta, ln0_g, ln0_b, w_in, ssd_conv_w, ssd_conv_b, ssd_dt_bias, ssd_a_log, ssd_d,
           ssd_norm_w, ssd_out, cf_dw_w, cf_dw_b, cf_ln_g, cf_ln_b, cf_out, w_o, ln1_g, ln1_b,
           peer_wq, peer_keys, peer_u, peer_v, ln2_g, ln2_b):
    bsz, seq, d = x.shape
    assert d == D_MODEL and seq % CHUNK == 0
    assert w_in.shape[0] == 1, "single layer"
    t = bsz * seq
    tp = t + bsz * CHUNK
    ln_rows = _divisor_tile(t, 512, 2 * CHUNK)
    assert bsz * CHUNK <= ln_rows

    rows = _divisor_tile(seq, 256, CHUNK)
    tm_proj = _divisor_tile(tp, 1664, SUBLANES)
    tm_merge = _divisor_tile(t, 256, SUBLANES)
    tq = _divisor_tile(t, 512, LANES)
    tm_peer = _divisor_tile(t, 512, 2 * BUILD_UNROLL)

    h0, h0b = _ln0_call(x.reshape(t, d), meta, ln0_g, ln0_b, tp, ln_rows)

    w = w_in[0]
    wz = w[:, :OFF_Z].astype(BF16)
    wxbc = w[:, OFF_Z:OFF_XBC].astype(BF16)
    wdt = jnp.pad(w[:, OFF_XBC:OFF_DT], ((0, 0), (0, LANES - SSD_HEADS))).astype(BF16)
    wcv =w[:, OFF_DT:OFF_DT + CF_WIDTH].astype(BF16)
    wcg = w[:, OFF_DT + CF_WIDTH:OFF_CF].astype(BF16)
    wgate = w[:, OFF_CF:].astype(BF16)
    pad_heads = lambda a: jnp.pad(a.astype(F32), (0, LANES - SSD_HEADS)).reshape(1, LANES)
    head_expand = jnp.tile(jnp.arange(SSD_D_INNER)[None, :] // SSD_HEADDIM
                           == jnp.arange(LANES)[:, None], (3, 1)).astype(BF16)

    zs = _proj_call(h0b, [wz], _silu, BF16, tm_proj, 512, name="proj_z")
    xbc = _proj_call(h0b, [wxbc], lambda a: a, BF16, tm_proj, 512, name="proj_xbc")
    dt = _proj_call(h0b, [wdt], _softplus, F32, tm_proj, LANES, bias=pad_heads(ssd_dt_bias[0]),
                    name="proj_dt")
    gates = _proj_call(h0b, [wgate], jax.nn.sigmoid, BF16, tm_proj, 512, name="proj_gate")

    yn = _ssd_call(xbc, zs, dt, ssd_conv_w[0], ssd_conv_b[0].reshape(1, -1), pad_heads(ssd_a_log[0]),
                   jnp.repeat(ssd_d[0].astype(F32), SSD_HEADDIM).reshape(1, -1),
                   ssd_norm_w[0].reshape(1, -1), head_expand, bsz, seq, rows)
    c2 = _conf_call(h0b, wcv, wcg, cf_dw_w[0], cf_dw_b[0].reshape(1, -1), cf_ln_g[0].reshape(1, -1),
                    cf_ln_b[0].reshape(1, -1), bsz, seq, rows)
    h1, h1b = _merge_call(yn, c2, gates, h0, ssd_out[0].astype(BF16), cf_out[0].astype(BF16),
                          w_o[0].astype(BF16), ln1_g[0].reshape(1, -1), ln1_b[0].reshape(1, -1),
                          tm_merge)

    i1, i2, gate = _peer_query_call(h1b, peer_wq[0].astype(BF16), peer_keys[0].astype(BF16), tq)
    out = _peer_expert_call(h1b, h1, i1, i2, gate, peer_u[0].astype(BF16), peer_v[0].astype(BF16),
                            ln2_g[0].reshape(1, -1), ln2_b[0].reshape(1, -1), tm_peer, 1024)
    return out.reshape(bsz, seq, d)
```

```python
import functools
import math

import jax
import jax.numpy as jnp
from jax import lax
from jax.experimental import pallas as pl
from jax.experimental.pallas import tpu as pltpu

F32 = jnp.float32
BF16 = jnp.bfloat16

D_MODEL = 2048
CHUNK = 64
N_META = 16
PAD_ROWS = CHUNK - N_META
SSD_D_INNER = 2048
SSD_HEADDIM = 64
SSD_HEADS = 32
SSD_GROUPS = 8
SSD_STATE = 128
SSD_CONV = 4
SSD_XBC = SSD_D_INNER + 2 * SSD_GROUPS * SSD_STATE
GROUP_COLS = SSD_D_INNER // SSD_GROUPS
HEADS_PER_GROUP = SSD_HEADS // SSD_GROUPS
CF_WIDTH = 2048
CF_CONV = 31
CF_HALO = 32
PEER_HEADS = 8
PEER_NKEYS = 128
PEER_EXPERTS = PEER_NKEYS * PEER_NKEYS
PEER_DKEY = 256
PEER_TOPK = 16
OFF_Z = SSD_D_INNER
OFF_XBC = OFF_Z + SSD_XBC
OFF_DT = OFF_XBC + SSD_HEADS
OFF_CF = OFF_DT + 2 * CF_WIDTH
LN_EPS = 1e-5
DN_ALPHA = 2.0 ** 0.25
LANES = 128
SUBLANES = 8
VMEM_LIMIT = 56 * 1024 * 1024
PEER_VMEM_LIMIT = 60 * 1024 * 1024

_CAND = [(a, b) for a in range(PEER_TOPK) for b in range(PEER_TOPK)
         if (a + 1) * (b + 1) <= PEER_TOPK]


def _cparams(sem, vmem_limit=VMEM_LIMIT):
    return pltpu.CompilerParams(dimension_semantics=sem, vmem_limit_bytes=vmem_limit)


def _resident(shape):
    nd = len(shape)
    return pl.BlockSpec(shape, lambda *_: (0,) * nd, pipeline_mode=pl.Buffered(1))


def _divisor_tile(n, target, mult):
    best = None
    for t in range(mult, min(n, target) + 1, mult):
        if n % t == 0:
            best = t
    assert best is not None, (n, target, mult)
    return best


def _layer_norm(x, g, b):
    mu = jnp.mean(x, axis=-1, keepdims=True)
    xc = x - mu
    var = jnp.mean(xc * xc, axis=-1, keepdims=True)
    return xc * lax.rsqrt(var + LN_EPS) * g + b


def _silu(x):
    return x * jax.nn.sigmoid(x)


def _ln0_kernel(x_ref, meta_ref, g_ref, b_ref, o_ref, ob_ref, *, n_token_tiles):
    i = pl.program_id(0)

    def emit(rows):
        y = _layer_norm(rows, g_ref[...], b_ref[...])
        o_ref[...] = y
        ob_ref[...] = y.astype(BF16)

    @pl.when(i < n_token_tiles)
    def _():
        emit(x_ref[...])

    @pl.when(i >= n_token_tiles)
    def _():
        head = jnp.concatenate([jnp.zeros((PAD_ROWS, D_MODEL), F32), meta_ref[...]], axis=0)
        emit(jnp.concatenate([head] * (x_ref.shape[0] // CHUNK), axis=0))


def _ln0_call(x2d, meta, g, b, tp, rows):
    t, d = x2d.shape
    n_tok = t // rows
    return pl.pallas_call(
        functools.partial(_ln0_kernel, n_token_tiles=n_tok),
        grid=(pl.cdiv(tp, rows),),
        in_specs=[pl.BlockSpec((rows, d), lambda i: (jnp.minimum(i, n_tok - 1), 0)),
                  pl.BlockSpec((N_META, d), lambda i: (0, 0)),
                  pl.BlockSpec((1, d), lambda i: (0, 0)),
                  pl.BlockSpec((1, d), lambda i: (0, 0))],
        out_specs=[pl.BlockSpec((rows, d), lambda i: (i, 0)),
                   pl.BlockSpec((rows, d), lambda i: (i, 0))],
        out_shape=[jax.ShapeDtypeStruct((tp, d), F32), jax.ShapeDtypeStruct((tp, d), BF16)],
        compiler_params=_cparams(("parallel",)),
        name="ln0",
    )(x2d, meta, g.reshape(1, d), b.reshape(1, d))


def _proj_kernel(x_ref, *refs, n_w, epilogue, has_bias):
    w_refs = refs[:n_w]
    b_ref = refs[n_w] if has_bias else None
    o_ref = refs[-1]
    x = x_ref[...]
    accs = [jnp.dot(x, w[...], preferred_element_type=F32) for w in w_refs]
    if has_bias:
        accs[0] = accs[0] + b_ref[...]
    o_ref[...] = epilogue(*accs).astype(o_ref.dtype)


def _proj_call(x, ws, epilogue, out_dtype, tm, tn, bias=None, name="proj", window=None):
    tp, k = x.shape
    col0, n = (0, ws[0].shape[1]) if window is None else window
    assert col0 % tn == 0 and n % tn == 0
    blk0 = col0 // tn
    in_specs = [pl.BlockSpec((tm, k), lambda i, j: (i, 0))]
    in_specs += [pl.BlockSpec((k, tn), lambda i, j: (0, blk0 + j)) for _ in ws]
    args = [x, *ws]
    if bias is not None:
        in_specs.append(pl.BlockSpec((1, tn), lambda i, j: (0, j)))
        args.append(bias)
    return pl.pallas_call(
        functools.partial(_proj_kernel, n_w=len(ws), epilogue=epilogue, has_bias=bias is not None),
        grid=(tp // tm, n // tn),
        in_specs=in_specs,
        out_specs=pl.BlockSpec((tm, tn), lambda i, j: (i, j)),
        out_shape=jax.ShapeDtypeStruct((tp, n), out_dtype),
        compiler_params=_cparams(("parallel", "parallel")),
        name=name,
    )(*args)


def _softplus(x):
    return jnp.maximum(x, 0.0) + jnp.log1p(jnp.exp(-jnp.abs(x)))


def _split3(x):
    hi = x.astype(BF16)
    r1 = x - hi.astype(F32)
    mid = r1.astype(BF16)
    lo = (r1 - mid.astype(F32)).astype(BF16)
    return jnp.concatenate([hi, mid, lo], axis=1)


def _ssd_block(xin, dt, n, refs, y_ref):
    cw_ref, cb_ref, alog_ref, hexp_ref, ext_ref, state_ref = refs
    hist = SUBLANES
    ext_ref[hist:hist + n, :] = xin
    acc = cb_ref[...] + cw_ref[0:1, :] * ext_ref[pl.ds(hist - (SSD_CONV - 1), n), :]
    for k in range(1, SSD_CONV):
        acc = acc + cw_ref[k:k + 1, :] * ext_ref[pl.ds(hist - (SSD_CONV - 1) + k, n), :]
    ext_ref[0:hist, :] = ext_ref[n:n + hist, :]
    xc = _silu(acc)
    xs = xc[:, :SSD_D_INNER]
    bm = xc[:, SSD_D_INNER:SSD_D_INNER + SSD_GROUPS * SSD_STATE].astype(BF16)
    cm = xc[:, SSD_D_INNER + SSD_GROUPS * SSD_STATE:].astype(BF16)

    da = dt * (-jnp.exp(alog_ref[...]))
    pos = lax.broadcasted_iota(jnp.int32, (n, 1), 0) % CHUNK
    cs = da
    shift = 1
    while shift < CHUNK:
        cs = cs + jnp.where(pos >= shift, pltpu.roll(cs, shift, axis=0), 0.0)
        shift *= 2
    hexp = hexp_ref[...]
    dt_e = jnp.dot(_split3(dt), hexp, preferred_element_type=F32)
    cs_e = jnp.dot(_split3(cs), hexp, preferred_element_type=F32)
    xdt = xs * dt_e

    sub = lax.broadcasted_iota(jnp.int32, (CHUNK, SSD_D_INNER), 0)
    lane_pos = lax.broadcasted_iota(jnp.int32, (CHUNK, SSD_D_INNER), 1) % SSD_HEADDIM
    diag = sub == lane_pos
    causal = sub >= lane_pos
    blk_r = lax.broadcasted_iota(jnp.int32, (GROUP_COLS, GROUP_COLS), 0) // SSD_HEADDIM
    blk_c = lax.broadcasted_iota(jnp.int32, (GROUP_COLS, GROUP_COLS), 1) // SSD_HEADDIM
    same_head = blk_r == blk_c

    for c in range(n // CHUNK):
        sl = slice(c * CHUNK, (c + 1) * CHUNK)
        cs_c = cs_e[sl]
        cs_end = cs_c[CHUNK - 1:CHUNK, :]
        decay_out = jnp.exp(cs_end - cs_c)
        decay_chunk = jnp.exp(cs_end)
        if y_ref is not None:
            cs_row = jnp.sum(jnp.where(diag, cs_c, 0.0), axis=0, keepdims=True)
            decay_l = jnp.where(causal, jnp.exp(cs_c - cs_row), 0.0)
            decay_in = jnp.exp(cs_c)
        for g in range(SSD_GROUPS):
            gc = slice(g * GROUP_COLS, (g + 1) * GROUP_COLS)
            gn = slice(g * SSD_STATE, (g + 1) * SSD_STATE)
            bm_g = bm[sl, gn]
            x_g = xdt[sl, gc]
            st = state_ref[g]
            if y_ref is not None:
                cm_g = cm[sl, gn]
                bm_rep = jnp.concatenate([bm_g] * HEADS_PER_GROUP, axis=0)
                scores = lax.dot_general(cm_g, bm_rep, (((1,), (1,)), ((), ())),
                                         preferred_element_type=F32)
                m = (scores * decay_l[:, gc]).astype(BF16)
                x_rep = jnp.concatenate([x_g] * HEADS_PER_GROUP, axis=0)
                x_bd = jnp.where(same_head, x_rep, 0.0).astype(BF16)
                y_diag = jnp.dot(m, x_bd, preferred_element_type=F32)
                y_off = jnp.dot(cm_g, st.astype(BF16), preferred_element_type=F32) * decay_in[:, gc]
                y_ref[sl, gc] = y_diag + y_off
            xd = (x_g * decay_out[:, gc]).astype(BF16)
            upd = lax.dot_general(bm_g, xd, (((0,), (0,)), ((), ())), preferred_element_type=F32)
            state_ref[g] = st * decay_chunk[:, gc] + upd
    return xs


def _ssd_kernel(xbc_ref, zs_ref, dt_ref, xbc_head_ref, dt_head_ref, cw_ref, cb_ref, alog_ref,
                dskip_ref, nw_ref, hexp_ref, o_ref, ext_ref, state_ref, y_ref, *, rows):
    refs = (cw_ref, cb_ref, alog_ref, hexp_ref, ext_ref, state_ref)

    @pl.when(pl.program_id(1) == 0)
    def _():
        ext_ref[0:SUBLANES, :] = jnp.zeros((SUBLANES, SSD_XBC), F32)
        state_ref[...] = jnp.zeros(state_ref.shape, F32)
        meta_row = lax.broadcasted_iota(jnp.int32, (CHUNK, 1), 0) >= PAD_ROWS
        _ssd_block(jnp.where(meta_row, xbc_head_ref[...].astype(F32), 0.0),
                   jnp.where(meta_row, dt_head_ref[...], 0.0), CHUNK, refs, None)

    xs = _ssd_block(xbc_ref[...].astype(F32), dt_ref[...], rows, refs, y_ref)
    y = y_ref[...] + dskip_ref[...] * xs
    yf = y * zs_ref[...].astype(F32)
    for g in range(SSD_GROUPS):
        gc = slice(g * GROUP_COLS, (g + 1) * GROUP_COLS)
        seg = yf[:, gc]
        ms = jnp.mean(seg * seg, axis=-1, keepdims=True)
        o_ref[:, gc] = (seg * lax.rsqrt(ms + LN_EPS) * nw_ref[:, gc]).astype(o_ref.dtype)


def _ssd_call(xbc, zs, dt, cw, cb, alog, dskip, nw, hexp, bsz, seq, rows):
    nblk = seq // rows
    head0 = bsz * seq // CHUNK
    rmap = lambda b, i: (b * nblk + i, 0)
    hmap = lambda b, i: (head0 + b, 0)
    cmap = lambda b, i: (0, 0)
    return pl.pallas_call(
        functools.partial(_ssd_kernel, rows=rows),
        grid=(bsz, nblk),
        in_specs=[pl.BlockSpec((rows, SSD_XBC), rmap),
                  pl.BlockSpec((rows, SSD_D_INNER), rmap),
                  pl.BlockSpec((rows, LANES), rmap),
                  pl.BlockSpec((CHUNK, SSD_XBC), hmap),
                  pl.BlockSpec((CHUNK, LANES), hmap),
                  pl.BlockSpec((SSD_CONV, SSD_XBC), cmap),
                  pl.BlockSpec((1, SSD_XBC), cmap),
                  pl.BlockSpec((1, LANES), cmap),
                  pl.BlockSpec((1, SSD_D_INNER), cmap),
                  pl.BlockSpec((1, SSD_D_INNER), cmap),
                  pl.BlockSpec((3 * LANES, SSD_D_INNER), cmap)],
        out_specs=pl.BlockSpec((rows, SSD_D_INNER), rmap),
        out_shape=jax.ShapeDtypeStruct((bsz * seq, SSD_D_INNER), BF16),
        scratch_shapes=[pltpu.VMEM((SUBLANES + rows, SSD_XBC), F32),
                        pltpu.VMEM((SSD_GROUPS, SSD_STATE, GROUP_COLS), F32),
                        pltpu.VMEM((rows, SSD_D_INNER), F32)],
        compiler_params=_cparams(("arbitrary", "arbitrary")),
        name="ssd",
    )(xbc, zs, dt, xbc, dt, cw, cb, alog, dskip, nw, hexp)


CF_ROW_BLK = 64
CF_COL_BLK = 256


def _glu_proj(x, wv_ref, wg_ref, cols):
    val = jnp.dot(x, wv_ref[:, cols], preferred_element_type=F32)
    gate = jnp.dot(x, wg_ref[:, cols], preferred_element_type=F32)
    return val * jax.nn.sigmoid(gate)


def _conf_kernel(x0_ref, xnext_ref, xhead_ref, wv_ref, wg_ref, w_ref, b_ref, g_ref, beta_ref, o_ref,
                 ext_ref, sh_ref, acc_ref, cnext_ref, *, rows):
    col_blocks = CF_WIDTH // CF_COL_BLK

    @pl.when(pl.program_id(1) == 0)
    def _():
        meta_row = lax.broadcasted_iota(jnp.int32, (CF_HALO, 1), 0) >= CF_HALO - N_META
        x_tail = xhead_ref[CHUNK - CF_HALO:CHUNK, :]
        for cb in range(col_blocks):
            cols = slice(cb * CF_COL_BLK, (cb + 1) * CF_COL_BLK)
            ext_ref[0:CF_HALO, cols] = jnp.where(meta_row, _glu_proj(x_tail, wv_ref, wg_ref, cols), 0.0)
            cnext_ref[:, cols] = _glu_proj(x0_ref[...], wv_ref, wg_ref, cols)

    ext_ref[CF_HALO:CF_HALO + rows, :] = cnext_ref[...]
    span = rows + CF_HALO - SUBLANES
    for s in range(1, SUBLANES):
        sh_ref[s - 1] = ext_ref[pl.ds(s, span), :]
    base = CF_HALO - (CF_CONV - 1)

    def col_block(cb, carry):
        cols = pl.ds(pl.multiple_of(cb * CF_COL_BLK, CF_COL_BLK), CF_COL_BLK)
        nxt = _glu_proj(xnext_ref[...], wv_ref, wg_ref, cols)
        for rb in range(rows // CF_ROW_BLK):
            r0 = rb * CF_ROW_BLK
            acc = jnp.broadcast_to(b_ref[:, cols], (CF_ROW_BLK, CF_COL_BLK))
            for k in range(CF_CONV):
                s = (base + k) % SUBLANES
                q = r0 + (base + k - s)
                if s == 0:
                    src = ext_ref[q:q + CF_ROW_BLK, cols]
                else:
                    src = sh_ref[s - 1, q:q + CF_ROW_BLK, cols]
                acc = acc + w_ref[k:k + 1, cols] * src
            acc_ref[r0:r0 + CF_ROW_BLK, cols] = acc
        cnext_ref[:, cols] = nxt
        return carry

    lax.fori_loop(0, col_blocks, col_block, 0)
    ext_ref[0:CF_HALO, :] = ext_ref[rows:rows + CF_HALO, :]
    o_ref[...] = _silu(_layer_norm(acc_ref[...], g_ref[...], beta_ref[...])).astype(o_ref.dtype)


def _conf_call(hb, wv, wg, w, b, g, beta, bsz, seq, rows):
    nblk = seq // rows
    head0 = bsz * seq // CHUNK
    rmap = lambda bb, i: (bb * nblk + i, 0)
    first = lambda bb, i: (bb * nblk, 0)
    nxt = lambda bb, i: (bb * nblk + jnp.minimum(i + 1, nblk - 1), 0)
    hmap = lambda bb, i: (head0 + bb, 0)
    cmap = lambda bb, i: (0, 0)
    return pl.pallas_call(
        functools.partial(_conf_kernel, rows=rows),
        grid=(bsz, nblk),
        in_specs=[pl.BlockSpec((rows, D_MODEL), first),
                  pl.BlockSpec((rows, D_MODEL), nxt),
                  pl.BlockSpec((CHUNK, D_MODEL), hmap),
                  _resident((D_MODEL, CF_WIDTH)),
                  _resident((D_MODEL, CF_WIDTH)),
                  pl.BlockSpec((CF_CONV, CF_WIDTH), cmap),
                  pl.BlockSpec((1, CF_WIDTH), cmap),
                  pl.BlockSpec((1, CF_WIDTH), cmap),
                  pl.BlockSpec((1, CF_WIDTH), cmap)],
        out_specs=pl.BlockSpec((rows, CF_WIDTH), rmap),
        out_shape=jax.ShapeDtypeStruct((bsz * seq, CF_WIDTH), BF16),
        scratch_shapes=[pltpu.VMEM((CF_HALO + rows, CF_WIDTH), F32),
                        pltpu.VMEM((SUBLANES - 1, CF_HALO + rows - SUBLANES, CF_WIDTH), F32),
                        pltpu.VMEM((rows, CF_WIDTH), F32),
                        pltpu.VMEM((rows, CF_WIDTH), F32)],
        compiler_params=_cparams(("arbitrary", "arbitrary")),
        name="conformer",
    )(hb, hb, hb, wv, wg, w, b, g, beta)


def _merge_kernel(yn_ref, c_ref, gate_ref, h_ref, wssd_ref, wcf_ref, wo_ref, g_ref, b_ref,
                  o_ref, ob_ref):
    y_ssd = jnp.dot(yn_ref[...], wssd_ref[...], preferred_element_type=F32)
    y_cf = jnp.dot(c_ref[...], wcf_ref[...], preferred_element_type=F32)
    gates = gate_ref[...].astype(F32)
    mix = gates[:, :D_MODEL] * y_ssd + gates[:, D_MODEL:] * y_cf
    m = jnp.dot(mix.astype(BF16), wo_ref[...], preferred_element_type=F32)
    h1 = _layer_norm(DN_ALPHA * h_ref[...] + m, g_ref[...], b_ref[...])
    o_ref[...] = h1
    ob_ref[...] = h1.astype(BF16)


def _merge_call(yn, c, gates, h0, wssd, wcf, wo, g, b, tm):
    t = yn.shape[0]
    rmap = lambda i: (i, 0)
    return pl.pallas_call(
        _merge_kernel,
        grid=(t // tm,),
        in_specs=[pl.BlockSpec((tm, SSD_D_INNER), rmap),
                  pl.BlockSpec((tm, CF_WIDTH), rmap),
                  pl.BlockSpec((tm, 2 * D_MODEL), rmap),
                  pl.BlockSpec((tm, D_MODEL), rmap),
                  _resident((SSD_D_INNER, D_MODEL)),
                  _resident((CF_WIDTH, D_MODEL)),
                  _resident((D_MODEL, D_MODEL)),
                  _resident((1, D_MODEL)),
                  _resident((1, D_MODEL))],
        out_specs=[pl.BlockSpec((tm, D_MODEL), rmap), pl.BlockSpec((tm, D_MODEL), rmap)],
        out_shape=[jax.ShapeDtypeStruct((t, D_MODEL), F32),
                   jax.ShapeDtypeStruct((t, D_MODEL), BF16)],
        compiler_params=_cparams(("parallel",)),
        name="merge",
    )(yn, c, gates, h0, wssd, wcf, wo, g, b)


def _slab(ref, j):
    if isinstance(j, int):
        return ref.at[:, j * PEER_HEADS:(j + 1) * PEER_HEADS, :]
    return ref.at[:, pl.ds(pl.multiple_of(j * PEER_HEADS, PEER_HEADS), PEER_HEADS), :]


def _top16_major(s_ref, ix_ref, val_ref, idx_ref, n, ids=None):
    ids = list(range(n)) if ids is None else ids
    assert n % 2 == 0 and all(x < y for x, y in zip(ids, ids[1:]))
    none = ids[-1] + 1
    neg = jnp.float32(-jnp.inf)
    half = n // 2
    m0 = None
    for p in range(half):
        a = _slab(s_ref, p)[...]
        b = _slab(s_ref, p + half)[...]
        swap = b > a
        front = jnp.where(swap, b, a)
        _slab(s_ref, p)[...] = front
        _slab(s_ref, p + half)[...] = jnp.where(swap, a, b)
        _slab(ix_ref, p)[...] = jnp.where(swap, ids[p + half], ids[p])
        _slab(ix_ref, p + half)[...] = jnp.where(swap, ids[p], ids[p + half])
        m0 = front if m0 is None else jnp.maximum(m0, front)

    def body(it, m):
        sel = jnp.full(m.shape, none, jnp.int32)
        for p in range(half):
            sel = jnp.minimum(sel, jnp.where(_slab(s_ref, p)[...] == m, _slab(ix_ref, p)[...], none))
        _slab(val_ref, it)[...] = m
        _slab(idx_ref, it)[...] = sel
        nxt = jnp.full(m.shape, neg, F32)
        for p in range(half):
            hit = _slab(ix_ref, p)[...] == sel
            back = _slab(s_ref, p + half)[...]
            front = jnp.where(hit, back, _slab(s_ref, p)[...])
            _slab(s_ref, p)[...] = front
            _slab(s_ref, p + half)[...] = jnp.where(hit, neg, back)
            _slab(ix_ref, p)[...] = jnp.where(hit, _slab(ix_ref, p + half)[...], _slab(ix_ref, p)[...])
            nxt = jnp.maximum(nxt, front)
        return nxt

    lax.fori_loop(0, PEER_TOPK, body, m0)


def _peer_query_kernel(hb_ref, wq_ref, keys_ref, i1_ref, i2_ref, gate_ref,
                       s_ref, ix_ref, v1_ref, x1_ref, v2_ref, x2_ref, cand_ref, cix_ref, bs_ref, sel_ref,
                       e1_ref, e2_ref):
    tq = hb_ref.shape[0]
    lane_blocks = tq // LANES
    q = jnp.dot(hb_ref[...], wq_ref[...], preferred_element_type=F32).astype(BF16)
    half_cols = PEER_DKEY // 2
    for half, (v_ref, x_ref) in enumerate(((v1_ref, x1_ref), (v2_ref, x2_ref))):
        for h in range(PEER_HEADS):
            c0 = h * PEER_DKEY + half * half_cols
            scores = lax.dot_general(keys_ref[half, h], q[:, c0:c0 + half_cols],
                                     (((1,), (1,)), ((), ())), preferred_element_type=F32)
            for lb in range(lane_blocks):
                s_ref[lb, pl.ds(h, PEER_NKEYS, stride=PEER_HEADS), :] = (
                    scores[:, lb * LANES:(lb + 1) * LANES])
        _top16_major(s_ref, ix_ref, v_ref, x_ref, PEER_NKEYS)

    for ci, (a, b) in enumerate(_CAND):
        _slab(cand_ref, ci)[...] = _slab(v1_ref, a)[...] + _slab(v2_ref, b)[...]
    _top16_major(cand_ref, cix_ref, bs_ref, sel_ref, len(_CAND),
                 ids=[a * PEER_TOPK + b for a, b in _CAND])

    def key_ids(it, carry):
        flat = _slab(sel_ref, it)[...]
        k1 = lax.shift_right_logical(flat, int(math.log2(PEER_TOPK)))
        k2 = flat & (PEER_TOPK - 1)
        e1 = jnp.zeros(flat.shape, jnp.int32)
        e2 = jnp.zeros(flat.shape, jnp.int32)
        for r in range(PEER_TOPK):
            e1 = jnp.where(k1 == r, _slab(x1_ref, r)[...], e1)
            e2 = jnp.where(k2 == r, _slab(x2_ref, r)[...], e2)
        _slab(e1_ref, it)[...] = e1
        _slab(e2_ref, it)[...] = e2
        return carry

    lax.fori_loop(0, PEER_TOPK, key_ids, 0)

    for lb in range(lane_blocks):
        bs = bs_ref[lb].reshape(PEER_TOPK, PEER_HEADS, LANES)
        ex = jnp.exp(bs - bs[0:1])
        gate = ex / jnp.sum(ex, axis=0, keepdims=True)
        tok = slice(lb * LANES, (lb + 1) * LANES)
        gate_ref[tok, :] = gate.reshape(PEER_TOPK * PEER_HEADS, LANES).T
        i1_ref[tok, :] = e1_ref[lb].T
        i2_ref[tok, :] = e2_ref[lb].T


def _peer_query_call(hb, wq, keys, tq):
    t = hb.shape[0]
    nj = PEER_TOPK * PEER_HEADS
    rmap = lambda i: (i, 0)
    slabs = lambda n, dt: pltpu.VMEM((tq // LANES, n * PEER_HEADS, LANES), dt)
    return pl.pallas_call(
        _peer_query_kernel,
        grid=(t // tq,),
        in_specs=[pl.BlockSpec((tq, D_MODEL), rmap),
                  _resident((D_MODEL, PEER_HEADS * PEER_DKEY)),
                  _resident((2, PEER_HEADS, PEER_NKEYS, PEER_DKEY // 2))],
        out_specs=[pl.BlockSpec((tq, nj), rmap)] * 3,
        out_shape=[jax.ShapeDtypeStruct((t, nj), jnp.int32),
                   jax.ShapeDtypeStruct((t, nj), jnp.int32),
                   jax.ShapeDtypeStruct((t, nj), F32)],
        scratch_shapes=[slabs(PEER_NKEYS, F32), slabs(PEER_NKEYS, jnp.int32),
                        slabs(PEER_TOPK, F32), slabs(PEER_TOPK, jnp.int32),
                        slabs(PEER_TOPK, F32), slabs(PEER_TOPK, jnp.int32),
                        slabs(len(_CAND), F32), slabs(len(_CAND), jnp.int32),
                        slabs(PEER_TOPK, F32), slabs(PEER_TOPK, jnp.int32),
                        slabs(PEER_TOPK, jnp.int32), slabs(PEER_TOPK, jnp.int32)],
        compiler_params=_cparams(("parallel",)),
        name="peer_query",
    )(hb, wq, keys)


BUILD_UNROLL = 16


def _gelu(x):
    return 0.5 * x * (1.0 + lax.erf(x * (1.0 / math.sqrt(2.0))))


def _pack_bf16_pair(a, b):
    bits = lambda v: lax.bitcast_convert_type(v.astype(BF16).astype(F32), jnp.uint32)
    return bits(a) | (bits(b) >> 16)


def _unpack_bf16_pair(w):
    return (lax.bitcast_convert_type(w & jnp.uint32(0xFFFF0000), F32),
            lax.bitcast_convert_type(w << 16, F32))


def _route_rows(i1_row, i2_row, gate_row, key_iota):
    a_t = jnp.where(key_iota == i1_row, gate_row, 0.0).astype(BF16)
    b_t = jnp.where(key_iota == i2_row, 1.0, 0.0).astype(BF16)
    return lax.dot_general(a_t, b_t, (((1,), (1,)), ((), ())), preferred_element_type=F32)


def _peer_expert_kernel(hb_ref, h_ref, i1_ref, i2_ref, gate_ref, u_ref, v_ref, g_ref, b_ref,
                        o_ref, gs_ref, *, tm, te, stride):
    e = pl.program_id(1)
    half = tm // 2

    @pl.when(e == 0)
    def _():
        o_ref[...] = jnp.zeros(o_ref.shape, F32)
        key_iota = lax.broadcasted_iota(jnp.int32, (PEER_NKEYS, PEER_NKEYS), 0)

        def build(blk, carry):
            lo = pl.multiple_of(blk * BUILD_UNROLL, BUILD_UNROLL)
            hi = pl.multiple_of(half + blk * BUILD_UNROLL, BUILD_UNROLL)
            rows = [(r[pl.ds(lo, BUILD_UNROLL), :], r[pl.ds(hi, BUILD_UNROLL), :])
                    for r in (i1_ref, i2_ref, gate_ref)]
            for k in range(BUILD_UNROLL):
                g0 = _route_rows(rows[0][0][k:k + 1], rows[1][0][k:k + 1], rows[2][0][k:k + 1], key_iota)
                g1 = _route_rows(rows[0][1][k:k + 1], rows[1][1][k:k + 1], rows[2][1][k:k + 1], key_iota)
                gs_ref[pl.ds(lo + k, PEER_NKEYS, stride=stride), :] = _pack_bf16_pair(g0, g1)
            return carry

        lax.fori_loop(0, half // BUILD_UNROLL, build, 0)

    s = lax.dot_general(hb_ref[...], u_ref[...], (((1,), (1,)), ((), ())),
                        preferred_element_type=F32)
    slabs = te // PEER_NKEYS
    packed = jnp.concatenate(
        [gs_ref[pl.ds(pl.multiple_of((e * slabs + r) * stride, SUBLANES), half), :]
         for r in range(slabs)], axis=1)
    route = jnp.concatenate(_unpack_bf16_pair(packed), axis=0)
    act = (_gelu(s) * route).astype(BF16)
    o_ref[...] += jnp.dot(act, v_ref[...], preferred_element_type=F32)

    @pl.when(e == pl.num_programs(1) - 1)
    def _():
        o_ref[...] = _layer_norm(DN_ALPHA * h_ref[...] + o_ref[...], g_ref[...], b_ref[...])


def _peer_expert_call(hb, h, i1, i2, gate, u, v, g, b, tm, te):
    t = hb.shape[0]
    nj = PEER_TOPK * PEER_HEADS
    assert tm % (2 * BUILD_UNROLL) == 0
    stride = tm // 2 + SUBLANES
    rmap = lambda i, e: (i, 0)
    emap = lambda i, e: (e, 0)
    cmap = lambda i, e: (0, 0)
    once = dict(pipeline_mode=pl.Buffered(1))
    return pl.pallas_call(
        functools.partial(_peer_expert_kernel, tm=tm, te=te, stride=stride),
        grid=(t // tm, PEER_EXPERTS // te),
        in_specs=[pl.BlockSpec((tm, D_MODEL), rmap),
                  pl.BlockSpec((tm, D_MODEL), rmap, **once),
                  pl.BlockSpec((tm, nj), rmap),
                  pl.BlockSpec((tm, nj), rmap),
                  pl.BlockSpec((tm, nj), rmap),
                  pl.BlockSpec((te, D_MODEL), emap),
                  pl.BlockSpec((te, D_MODEL), emap),
                  pl.BlockSpec((1, D_MODEL), cmap),
                  pl.BlockSpec((1, D_MODEL), cmap)],
        out_specs=pl.BlockSpec((tm, D_MODEL), rmap),
        out_shape=jax.ShapeDtypeStruct((t, D_MODEL), F32),
        scratch_shapes=[pltpu.VMEM((PEER_NKEYS * stride, PEER_NKEYS), jnp.uint32)],
        compiler_params=_cparams(("parallel", "arbitrary"), PEER_VMEM_LIMIT),
        name="peer_experts",
    )(hb, h, i1, i2, gate, u, v, g, b)


def kernel(x, meta, ln0_g, ln0_b, w_in, ssd_conv_w, ssd_conv_b, ssd_dt_bias, ssd_a_log, ssd_d,
           ssd_norm_w, ssd_out, cf_dw_w, cf_dw_b, cf_ln_g, cf_ln_b, cf_out, w_o, ln1_g, ln1_b,
           peer_wq, peer_keys, peer_u, peer_v, ln2_g, ln2_b):
    bsz, seq, d = x.shape
    assert d == D_MODEL and seq % CHUNK == 0
    assert w_in.shape[0] == 1, "single layer"
    t = bsz * seq
    tp = t + bsz * CHUNK
    ln_rows = _divisor_tile(t, 512, 2 * CHUNK)
    assert bsz * CHUNK <= ln_rows

    rows = _divisor_tile(seq, 256, CHUNK)
    tm_proj = _divisor_tile(tp, 1664, SUBLANES)
    tm_merge = _divisor_tile(t, 256, SUBLANES)
    tq = _divisor_tile(t, 512, LANES)
    tm_peer = _divisor_tile(t, 512, 2 * BUILD_UNROLL)

    h0, h0b = _ln0_call(x.reshape(t, d), meta, ln0_g, ln0_b, tp, ln_rows)

    w = w_in[0].astype(BF16)
    wcv = w[:, OFF_DT:OFF_DT + CF_WIDTH]
    wcg = w[:, OFF_DT + CF_WIDTH:OFF_CF]
    wgate = w[:, OFF_CF:]
    pad_heads = lambda a: jnp.pad(a.astype(F32), (0, LANES - SSD_HEADS)).reshape(1, LANES)
    head_expand = jnp.tile(jnp.arange(SSD_D_INNER)[None, :] // SSD_HEADDIM
                           == jnp.arange(LANES)[:, None], (3, 1)).astype(BF16)

    zs = _proj_call(h0b, [w], _silu, BF16, tm_proj, 512, name="proj_z", window=(0, OFF_Z))
    xbc = _proj_call(h0b, [w], lambda a: a, BF16, tm_proj, 512, name="proj_xbc",
                     window=(OFF_Z, SSD_XBC))
    dt = _proj_call(h0b, [w], _softplus, F32, tm_proj, LANES, bias=pad_heads(ssd_dt_bias[0]),
                    name="proj_dt", window=(OFF_XBC, LANES))
    gates = _proj_call(h0b, [wgate], jax.nn.sigmoid, BF16, tm_proj, 512, name="proj_gate")

    yn = _ssd_call(xbc, zs, dt, ssd_conv_w[0], ssd_conv_b[0].reshape(1, -1), pad_heads(ssd_a_log[0]),
                   jnp.repeat(ssd_d[0].astype(F32), SSD_HEADDIM).reshape(1, -1),
                   ssd_norm_w[0].reshape(1, -1), head_expand, bsz, seq, rows)
    c2 = _conf_call(h0b, wcv, wcg, cf_dw_w[0], cf_dw_b[0].reshape(1, -1), cf_ln_g[0].reshape(1, -1),
                    cf_ln_b[0].reshape(1, -1), bsz, seq, rows)
    h1, h1b = _merge_call(yn, c2, gates, h0, ssd_out[0].astype(BF16), cf_out[0].astype(BF16),
                          w_o[0].astype(BF16), ln1_g[0].reshape(1, -1), ln1_b[0].reshape(1, -1),
                          tm_merge)

    i1, i2, gate = _peer_query_call(h1b, peer_wq[0].astype(BF16), peer_keys[0].astype(BF16), tq)
    out = _peer_expert_call(h1b, h1, i1, i2, gate, peer_u[0].astype(BF16), peer_v[0].astype(BF16),
                            ln2_g[0].reshape(1, -1), ln2_b[0].reshape(1, -1), tm_peer, 1024)
    return out.reshape(bsz, seq, d)
```

```python
import functools
import math

import jax
import jax.numpy as jnp
from jax import lax
from jax.experimental import pallas as pl
from jax.experimental.pallas import tpu as pltpu

F32 = jnp.float32
BF16 = jnp.bfloat16

D_MODEL = 2048
CHUNK = 64
N_META = 16
PAD_ROWS = CHUNK - N_META
SSD_D_INNER = 2048
SSD_HEADDIM = 64
SSD_HEADS = 32
SSD_GROUPS = 8
SSD_STATE = 128
SSD_CONV = 4
SSD_XBC = SSD_D_INNER + 2 * SSD_GROUPS * SSD_STATE
GROUP_COLS = SSD_D_INNER // SSD_GROUPS
HEADS_PER_GROUP = SSD_HEADS // SSD_GROUPS
CF_WIDTH = 2048
CF_CONV = 31
CF_HALO = 32
PEER_HEADS = 8
PEER_NKEYS = 128
PEER_EXPERTS = PEER_NKEYS * PEER_NKEYS
PEER_DKEY = 256
PEER_TOPK = 16
OFF_Z = SSD_D_INNER
OFF_XBC = OFF_Z + SSD_XBC
OFF_DT = OFF_XBC + SSD_HEADS
OFF_CF = OFF_DT + 2 * CF_WIDTH
LN_EPS = 1e-5
DN_ALPHA = 2.0 ** 0.25
LANES = 128
SUBLANES = 8
VMEM_LIMIT = 56 * 1024 * 1024
PEER_VMEM_LIMIT = 60 * 1024 * 1024

_CAND = [(a, b) for a in range(PEER_TOPK) for b in range(PEER_TOPK)
         if (a + 1) * (b + 1) <= PEER_TOPK]


def _cparams(sem, vmem_limit=VMEM_LIMIT):
    return pltpu.CompilerParams(dimension_semantics=sem, vmem_limit_bytes=vmem_limit)


def _resident(shape):
    nd = len(shape)
    return pl.BlockSpec(shape, lambda *_: (0,) * nd, pipeline_mode=pl.Buffered(1))


def _divisor_tile(n, target, mult):
    best = None
    for t in range(mult, min(n, target) + 1, mult):
        if n % t == 0:
            best = t
    assert best is not None, (n, target, mult)
    return best


def _layer_norm(x, g, b):
    mu = jnp.mean(x, axis=-1, keepdims=True)
    xc = x - mu
    var = jnp.mean(xc * xc, axis=-1, keepdims=True)
    return xc * lax.rsqrt(var + LN_EPS) * g + b


def _silu(x):
    return x * jax.nn.sigmoid(x)


def _ln0_kernel(x_ref, meta_ref, g_ref, b_ref, o_ref, ob_ref, *, n_token_tiles):
    i = pl.program_id(0)

    def emit(rows):
        y = _layer_norm(rows, g_ref[...], b_ref[...])
        o_ref[...] = y
        ob_ref[...] = y.astype(BF16)

    @pl.when(i < n_token_tiles)
    def _():
        emit(x_ref[...])

    @pl.when(i >= n_token_tiles)
    def _():
        head = jnp.concatenate([jnp.zeros((PAD_ROWS, D_MODEL), F32), meta_ref[...]], axis=0)
        emit(jnp.concatenate([head] * (x_ref.shape[0] // CHUNK), axis=0))


def _ln0_call(x2d, meta, g, b, tp, rows):
    t, d = x2d.shape
    n_tok = t // rows
    return pl.pallas_call(
        functools.partial(_ln0_kernel, n_token_tiles=n_tok),
        grid=(pl.cdiv(tp, rows),),
        in_specs=[pl.BlockSpec((rows, d), lambda i: (jnp.minimum(i, n_tok - 1), 0)),
                  pl.BlockSpec((N_META, d), lambda i: (0, 0)),
                  pl.BlockSpec((1, d), lambda i: (0, 0)),
                  pl.BlockSpec((1, d), lambda i: (0, 0))],
        out_specs=[pl.BlockSpec((rows, d), lambda i: (i, 0)),
                   pl.BlockSpec((rows, d), lambda i: (i, 0))],
        out_shape=[jax.ShapeDtypeStruct((tp, d), F32), jax.ShapeDtypeStruct((tp, d), BF16)],
        compiler_params=_cparams(("parallel",)),
        name="ln0",
    )(x2d, meta, g.reshape(1, d), b.reshape(1, d))


def _proj_kernel(x_ref, *refs, n_w, epilogue, has_bias):
    w_refs = refs[:n_w]
    b_ref = refs[n_w] if has_bias else None
    o_ref = refs[-1]
    x = x_ref[...]
    accs = [jnp.dot(x, w[...], preferred_element_type=F32) for w in w_refs]
    if has_bias:
        accs[0] = accs[0] + b_ref[...]
    o_ref[...] = epilogue(*accs).astype(o_ref.dtype)


def _proj_call(x, ws, epilogue, out_dtype, tm, tn, bias=None, name="proj", window=None):
    tp, k = x.shape
    col0, n = (0, ws[0].shape[1]) if window is None else window
    assert col0 % tn == 0 and n % tn == 0
    blk0 = col0 // tn
    in_specs = [pl.BlockSpec((tm, k), lambda i, j: (i, 0))]
    in_specs += [pl.BlockSpec((k, tn), lambda i, j: (0, blk0 + j)) for _ in ws]
    args = [x, *ws]
    if bias is not None:
        in_specs.append(pl.BlockSpec((1, tn), lambda i, j: (0, j)))
        args.append(bias)
    return pl.pallas_call(
        functools.partial(_proj_kernel, n_w=len(ws), epilogue=epilogue, has_bias=bias is not None),
        grid=(tp // tm, n // tn),
        in_specs=in_specs,
        out_specs=pl.BlockSpec((tm, tn), lambda i, j: (i, j)),
        out_shape=jax.ShapeDtypeStruct((tp, n), out_dtype),
        compiler_params=_cparams(("parallel", "parallel")),
        name=name,
    )(*args)


def _softplus(x):
    return jnp.maximum(x, 0.0) + jnp.log1p(jnp.exp(-jnp.abs(x)))


def _split3(x):
    hi = x.astype(BF16)
    r1 = x - hi.astype(F32)
    mid = r1.astype(BF16)
    lo = (r1 - mid.astype(F32)).astype(BF16)
    return jnp.concatenate([hi, mid, lo], axis=1)


def _ssd_block(xin, dt, n, refs, y_ref):
    cw_ref, cb_ref, alog_ref, hexp_ref, ext_ref, state_ref = refs
    hist = SUBLANES
    ext_ref[hist:hist + n, :] = xin
    acc = cb_ref[...] + cw_ref[0:1, :] * ext_ref[pl.ds(hist - (SSD_CONV - 1), n), :]
    for k in range(1, SSD_CONV):
        acc = acc + cw_ref[k:k + 1, :] * ext_ref[pl.ds(hist - (SSD_CONV - 1) + k, n), :]
    ext_ref[0:hist, :] = ext_ref[n:n + hist, :]
    xc = _silu(acc)
    xs = xc[:, :SSD_D_INNER]
    bm = xc[:, SSD_D_INNER:SSD_D_INNER + SSD_GROUPS * SSD_STATE].astype(BF16)
    cm = xc[:, SSD_D_INNER + SSD_GROUPS * SSD_STATE:].astype(BF16)

    da = dt * (-jnp.exp(alog_ref[...]))
    pos = lax.broadcasted_iota(jnp.int32, (n, 1), 0) % CHUNK
    cs = da
    shift = 1
    while shift < CHUNK:
        cs = cs + jnp.where(pos >= shift, pltpu.roll(cs, shift, axis=0), 0.0)
        shift *= 2
    hexp = hexp_ref[...]
    dt_e = jnp.dot(_split3(dt), hexp, preferred_element_type=F32)
    cs_e = jnp.dot(_split3(cs), hexp, preferred_element_type=F32)
    xdt = xs * dt_e

    sub = lax.broadcasted_iota(jnp.int32, (CHUNK, SSD_D_INNER), 0)
    lane_pos = lax.broadcasted_iota(jnp.int32, (CHUNK, SSD_D_INNER), 1) % SSD_HEADDIM
    diag = sub == lane_pos
    causal = sub >= lane_pos
    blk_r = lax.broadcasted_iota(jnp.int32, (GROUP_COLS, GROUP_COLS), 0) // SSD_HEADDIM
    blk_c = lax.broadcasted_iota(jnp.int32, (GROUP_COLS, GROUP_COLS), 1) // SSD_HEADDIM
    same_head = blk_r == blk_c

    for c in range(n // CHUNK):
        sl = slice(c * CHUNK, (c + 1) * CHUNK)
        cs_c = cs_e[sl]
        cs_end = cs_c[CHUNK - 1:CHUNK, :]
        decay_out = jnp.exp(cs_end - cs_c)
        decay_chunk = jnp.exp(cs_end)
        if y_ref is not None:
            cs_row = jnp.sum(jnp.where(diag, cs_c, 0.0), axis=0, keepdims=True)
            decay_l = jnp.where(causal, jnp.exp(cs_c - cs_row), 0.0)
            decay_in = jnp.exp(cs_c)
        for g in range(SSD_GROUPS):
            gc = slice(g * GROUP_COLS, (g + 1) * GROUP_COLS)
            gn = slice(g * SSD_STATE, (g + 1) * SSD_STATE)
            bm_g = bm[sl, gn]
            x_g = xdt[sl, gc]
            st = state_ref[g]
            if y_ref is not None:
                cm_g = cm[sl, gn]
                bm_rep = jnp.concatenate([bm_g] * HEADS_PER_GROUP, axis=0)
                scores = lax.dot_general(cm_g, bm_rep, (((1,), (1,)), ((), ())),
                                         preferred_element_type=F32)
                m = (scores * decay_l[:, gc]).astype(BF16)
                x_rep = jnp.concatenate([x_g] * HEADS_PER_GROUP, axis=0)
                x_bd = jnp.where(same_head, x_rep, 0.0).astype(BF16)
                y_diag = jnp.dot(m, x_bd, preferred_element_type=F32)
                y_off = jnp.dot(cm_g, st.astype(BF16), preferred_element_type=F32) * decay_in[:, gc]
                y_ref[sl, gc] = y_diag + y_off
            xd = (x_g * decay_out[:, gc]).astype(BF16)
            upd = lax.dot_general(bm_g, xd, (((0,), (0,)), ((), ())), preferred_element_type=F32)
            state_ref[g] = st * decay_chunk[:, gc] + upd
    return xs


def _ssd_kernel(xbc_ref, zs_ref, dt_ref, xbc_head_ref, dt_head_ref, cw_ref, cb_ref, alog_ref,
                dskip_ref, nw_ref, hexp_ref, o_ref, ext_ref, state_ref, y_ref, *, rows):
    refs = (cw_ref, cb_ref, alog_ref, hexp_ref, ext_ref, state_ref)

    @pl.when(pl.program_id(1) == 0)
    def _():
        ext_ref[0:SUBLANES, :] = jnp.zeros((SUBLANES, SSD_XBC), F32)
        state_ref[...] = jnp.zeros(state_ref.shape, F32)
        meta_row = lax.broadcasted_iota(jnp.int32, (CHUNK, 1), 0) >= PAD_ROWS
        _ssd_block(jnp.where(meta_row, xbc_head_ref[...].astype(F32), 0.0),
                   jnp.where(meta_row, dt_head_ref[...], 0.0), CHUNK, refs, None)

    xs = _ssd_block(xbc_ref[...].astype(F32), dt_ref[...], rows, refs, y_ref)
    y = y_ref[...] + dskip_ref[...] * xs
    yf = y * zs_ref[...].astype(F32)
    for g in range(SSD_GROUPS):
        gc = slice(g * GROUP_COLS, (g + 1) * GROUP_COLS)
        seg = yf[:, gc]
        ms = jnp.mean(seg * seg, axis=-1, keepdims=True)
        o_ref[:, gc] = (seg * lax.rsqrt(ms + LN_EPS) * nw_ref[:, gc]).astype(o_ref.dtype)


def _ssd_call(xbc, zs, dt, cw, cb, alog, dskip, nw, hexp, bsz, seq, rows):
    nblk = seq // rows
    head0 = bsz * seq // CHUNK
    rmap = lambda b, i: (b * nblk + i, 0)
    hmap = lambda b, i: (head0 + b, 0)
    cmap = lambda b, i: (0, 0)
    return pl.pallas_call(
        functools.partial(_ssd_kernel, rows=rows),
        grid=(bsz, nblk),
        in_specs=[pl.BlockSpec((rows, SSD_XBC), rmap),
                  pl.BlockSpec((rows, SSD_D_INNER), rmap),
                  pl.BlockSpec((rows, LANES), rmap),
                  pl.BlockSpec((CHUNK, SSD_XBC), hmap),
                  pl.BlockSpec((CHUNK, LANES), hmap),
                  pl.BlockSpec((SSD_CONV, SSD_XBC), cmap),
                  pl.BlockSpec((1, SSD_XBC), cmap),
                  pl.BlockSpec((1, LANES), cmap),
                  pl.BlockSpec((1, SSD_D_INNER), cmap),
                  pl.BlockSpec((1, SSD_D_INNER), cmap),
                  pl.BlockSpec((3 * LANES, SSD_D_INNER), cmap)],
        out_specs=pl.BlockSpec((rows, SSD_D_INNER), rmap),
        out_shape=jax.ShapeDtypeStruct((bsz * seq, SSD_D_INNER), BF16),
        scratch_shapes=[pltpu.VMEM((SUBLANES + rows, SSD_XBC), F32),
                        pltpu.VMEM((SSD_GROUPS, SSD_STATE, GROUP_COLS), F32),
                        pltpu.VMEM((rows, SSD_D_INNER), F32)],
        compiler_params=_cparams(("arbitrary", "arbitrary")),
        name="ssd",
    )(xbc, zs, dt, xbc, dt, cw, cb, alog, dskip, nw, hexp)


CF_ROW_BLK = 64
CF_COL_BLK = 256


def _glu_proj(x, wv_ref, wg_ref, cols):
    val = jnp.dot(x, wv_ref[:, cols], preferred_element_type=F32)
    gate = jnp.dot(x, wg_ref[:, cols], preferred_element_type=F32)
    return val * jax.nn.sigmoid(gate)


def _conf_kernel(x0_ref, xnext_ref, xhead_ref, wv_ref, wg_ref, w_ref, b_ref, g_ref, beta_ref, o_ref,
                 ext_ref, sh_ref, acc_ref, cnext_ref, *, rows):
    col_blocks = CF_WIDTH // CF_COL_BLK

    @pl.when(pl.program_id(1) == 0)
    def _():
        meta_row = lax.broadcasted_iota(jnp.int32, (CF_HALO, 1), 0) >= CF_HALO - N_META
        x_tail = xhead_ref[CHUNK - CF_HALO:CHUNK, :]
        for cb in range(col_blocks):
            cols = slice(cb * CF_COL_BLK, (cb + 1) * CF_COL_BLK)
            ext_ref[0:CF_HALO, cols] = jnp.where(meta_row, _glu_proj(x_tail, wv_ref, wg_ref, cols), 0.0)
            cnext_ref[:, cols] = _glu_proj(x0_ref[...], wv_ref, wg_ref, cols)

    ext_ref[CF_HALO:CF_HALO + rows, :] = cnext_ref[...]
    span = rows + CF_HALO - SUBLANES
    for s in range(1, SUBLANES):
        sh_ref[s - 1] = ext_ref[pl.ds(s, span), :]
    base = CF_HALO - (CF_CONV - 1)

    def col_block(cb, carry):
        cols = pl.ds(pl.multiple_of(cb * CF_COL_BLK, CF_COL_BLK), CF_COL_BLK)
        nxt = _glu_proj(xnext_ref[...], wv_ref, wg_ref, cols)
        for rb in range(rows // CF_ROW_BLK):
            r0 = rb * CF_ROW_BLK
            acc = jnp.broadcast_to(b_ref[:, cols], (CF_ROW_BLK, CF_COL_BLK))
            for k in range(CF_CONV):
                s = (base + k) % SUBLANES
                q = r0 + (base + k - s)
                if s == 0:
                    src = ext_ref[q:q + CF_ROW_BLK, cols]
                else:
                    src = sh_ref[s - 1, q:q + CF_ROW_BLK, cols]
                acc = acc + w_ref[k:k + 1, cols] * src
            acc_ref[r0:r0 + CF_ROW_BLK, cols] = acc
        cnext_ref[:, cols] = nxt
        return carry

    lax.fori_loop(0, col_blocks, col_block, 0)
    ext_ref[0:CF_HALO, :] = ext_ref[rows:rows + CF_HALO, :]
    o_ref[...] = _silu(_layer_norm(acc_ref[...], g_ref[...], beta_ref[...])).astype(o_ref.dtype)


def _conf_call(hb, wv, wg, w, b, g, beta, bsz, seq, rows):
    nblk = seq // rows
    head0 = bsz * seq // CHUNK
    rmap = lambda bb, i: (bb * nblk + i, 0)
    first = lambda bb, i: (bb * nblk, 0)
    nxt = lambda bb, i: (bb * nblk + jnp.minimum(i + 1, nblk - 1), 0)
    hmap = lambda bb, i: (head0 + bb, 0)
    cmap = lambda bb, i: (0, 0)
    return pl.pallas_call(
        functools.partial(_conf_kernel, rows=rows),
        grid=(bsz, nblk),
        in_specs=[pl.BlockSpec((rows, D_MODEL), first),
                  pl.BlockSpec((rows, D_MODEL), nxt),
                  pl.BlockSpec((CHUNK, D_MODEL), hmap),
                  _resident((D_MODEL, CF_WIDTH)),
                  _resident((D_MODEL, CF_WIDTH)),
                  pl.BlockSpec((CF_CONV, CF_WIDTH), cmap),
                  pl.BlockSpec((1, CF_WIDTH), cmap),
                  pl.BlockSpec((1, CF_WIDTH), cmap),
                  pl.BlockSpec((1, CF_WIDTH), cmap)],
        out_specs=pl.BlockSpec((rows, CF_WIDTH), rmap),
        out_shape=jax.ShapeDtypeStruct((bsz * seq, CF_WIDTH), BF16),
        scratch_shapes=[pltpu.VMEM((CF_HALO + rows, CF_WIDTH), F32),
                        pltpu.VMEM((SUBLANES - 1, CF_HALO + rows - SUBLANES, CF_WIDTH), F32),
                        pltpu.VMEM((rows, CF_WIDTH), F32),
                        pltpu.VMEM((rows, CF_WIDTH), F32)],
        compiler_params=_cparams(("arbitrary", "arbitrary")),
        name="conformer",
    )(hb, hb, hb, wv, wg, w, b, g, beta)


def _merge_kernel(yn_ref, c_ref, gate_ref, h_ref, wssd_ref, wcf_ref, wo_ref, g_ref, b_ref,
                  o_ref, ob_ref):
    y_ssd = jnp.dot(yn_ref[...], wssd_ref[...], preferred_element_type=F32)
    y_cf = jnp.dot(c_ref[...], wcf_ref[...], preferred_element_type=F32)
    gates = gate_ref[...].astype(F32)
    mix = gates[:, :D_MODEL] * y_ssd + gates[:, D_MODEL:] * y_cf
    m = jnp.dot(mix.astype(BF16), wo_ref[...], preferred_element_type=F32)
    h1 = _layer_norm(DN_ALPHA * h_ref[...] + m, g_ref[...], b_ref[...])
    o_ref[...] = h1
    ob_ref[...] = h1.astype(BF16)


def _merge_call(yn, c, gates, h0, wssd, wcf, wo, g, b, tm):
    t = yn.shape[0]
    rmap = lambda i: (i, 0)
    return pl.pallas_call(
        _merge_kernel,
        grid=(t // tm,),
        in_specs=[pl.BlockSpec((tm, SSD_D_INNER), rmap),
                  pl.BlockSpec((tm, CF_WIDTH), rmap),
                  pl.BlockSpec((tm, 2 * D_MODEL), rmap),
                  pl.BlockSpec((tm, D_MODEL), rmap),
                  _resident((SSD_D_INNER, D_MODEL)),
                  _resident((CF_WIDTH, D_MODEL)),
                  _resident((D_MODEL, D_MODEL)),
                  _resident((1, D_MODEL)),
                  _resident((1, D_MODEL))],
        out_specs=[pl.BlockSpec((tm, D_MODEL), rmap), pl.BlockSpec((tm, D_MODEL), rmap)],
        out_shape=[jax.ShapeDtypeStruct((t, D_MODEL), F32),
                   jax.ShapeDtypeStruct((t, D_MODEL), BF16)],
        compiler_params=_cparams(("parallel",)),
        name="merge",
    )(yn, c, gates, h0, wssd, wcf, wo, g, b)


def _slab(ref, j):
    if isinstance(j, int):
        return ref.at[:, j * PEER_HEADS:(j + 1) * PEER_HEADS, :]
    return ref.at[:, pl.ds(pl.multiple_of(j * PEER_HEADS, PEER_HEADS), PEER_HEADS), :]


def _top16_major(s_ref, ix_ref, val_ref, idx_ref, n, ids=None):
    ids = list(range(n)) if ids is None else ids
    assert n % 2 == 0 and all(x < y for x, y in zip(ids, ids[1:]))
    none = ids[-1] + 1
    neg = jnp.float32(-jnp.inf)
    half = n // 2
    m0 = None
    for p in range(half):
        a = _slab(s_ref, p)[...]
        b = _slab(s_ref, p + half)[...]
        swap = b > a
        front = jnp.where(swap, b, a)
        _slab(s_ref, p)[...] = front
        _slab(s_ref, p + half)[...] = jnp.where(swap, a, b)
        _slab(ix_ref, p)[...] = jnp.where(swap, ids[p + half], ids[p])
        _slab(ix_ref, p + half)[...] = jnp.where(swap, ids[p], ids[p + half])
        m0 = front if m0 is None else jnp.maximum(m0, front)

    def body(it, m):
        sel = jnp.full(m.shape, none, jnp.int32)
        for p in range(half):
            sel = jnp.minimum(sel, jnp.where(_slab(s_ref, p)[...] == m, _slab(ix_ref, p)[...], none))
        _slab(val_ref, it)[...] = m
        _slab(idx_ref, it)[...] = sel
        nxt = jnp.full(m.shape, neg, F32)
        for p in range(half):
            hit = _slab(ix_ref, p)[...] == sel
            back = _slab(s_ref, p + half)[...]
            front = jnp.where(hit, back, _slab(s_ref, p)[...])
            _slab(s_ref, p)[...] = front
            _slab(s_ref, p + half)[...] = jnp.where(hit, neg, back)
            _slab(ix_ref, p)[...] = jnp.where(hit, _slab(ix_ref, p + half)[...], _slab(ix_ref, p)[...])
            nxt = jnp.maximum(nxt, front)
        return nxt

    lax.fori_loop(0, PEER_TOPK, body, m0)


def _peer_query_kernel(hb_ref, wq_ref, keys_ref, i1_ref, i2_ref, gate_ref,
                       s_ref, ix_ref, v1_ref, x1_ref, v2_ref, x2_ref, cand_ref, cix_ref, bs_ref, sel_ref,
                       e1_ref, e2_ref):
    tq = hb_ref.shape[0]
    lane_blocks = tq // LANES
    q = jnp.dot(hb_ref[...], wq_ref[...], preferred_element_type=F32).astype(BF16)
    half_cols = PEER_DKEY // 2
    for half, (v_ref, x_ref) in enumerate(((v1_ref, x1_ref), (v2_ref, x2_ref))):
        for h in range(PEER_HEADS):
            c0 = h * PEER_DKEY + half * half_cols
            scores = lax.dot_general(keys_ref[half, h], q[:, c0:c0 + half_cols],
                                     (((1,), (1,)), ((), ())), preferred_element_type=F32)
            for lb in range(lane_blocks):
                s_ref[lb, pl.ds(h, PEER_NKEYS, stride=PEER_HEADS), :] = (
                    scores[:, lb * LANES:(lb + 1) * LANES])
        _top16_major(s_ref, ix_ref, v_ref, x_ref, PEER_NKEYS)

    for ci, (a, b) in enumerate(_CAND):
        _slab(cand_ref, ci)[...] = _slab(v1_ref, a)[...] + _slab(v2_ref, b)[...]
    _top16_major(cand_ref, cix_ref, bs_ref, sel_ref, len(_CAND),
                 ids=[a * PEER_TOPK + b for a, b in _CAND])

    def key_ids(it, carry):
        flat = _slab(sel_ref, it)[...]
        k1 = lax.shift_right_logical(flat, int(math.log2(PEER_TOPK)))
        k2 = flat & (PEER_TOPK - 1)
        e1 = jnp.zeros(flat.shape, jnp.int32)
        e2 = jnp.zeros(flat.shape, jnp.int32)
        for r in range(PEER_TOPK):
            e1 = jnp.where(k1 == r, _slab(x1_ref, r)[...], e1)
            e2 = jnp.where(k2 == r, _slab(x2_ref, r)[...], e2)
        _slab(e1_ref, it)[...] = e1
        _slab(e2_ref, it)[...] = e2
        return carry

    lax.fori_loop(0, PEER_TOPK, key_ids, 0)

    for lb in range(lane_blocks):
        bs = bs_ref[lb].reshape(PEER_TOPK, PEER_HEADS, LANES)
        ex = jnp.exp(bs - bs[0:1])
        gate = ex / jnp.sum(ex, axis=0, keepdims=True)
        tok = slice(lb * LANES, (lb + 1) * LANES)
        gate_ref[tok, :] = gate.reshape(PEER_TOPK * PEER_HEADS, LANES).T
        i1_ref[tok, :] = e1_ref[lb].T
        i2_ref[tok, :] = e2_ref[lb].T


def _peer_query_call(hb, wq, keys, tq):
    t = hb.shape[0]
    nj = PEER_TOPK * PEER_HEADS
    rmap = lambda i: (i, 0)
    slabs = lambda n, dt: pltpu.VMEM((tq // LANES, n * PEER_HEADS, LANES), dt)
    return pl.pallas_call(
        _peer_query_kernel,
        grid=(t // tq,),
        in_specs=[pl.BlockSpec((tq, D_MODEL), rmap),
                  _resident((D_MODEL, PEER_HEADS * PEER_DKEY)),
                  _resident((2, PEER_HEADS, PEER_NKEYS, PEER_DKEY // 2))],
        out_specs=[pl.BlockSpec((tq, nj), rmap)] * 3,
        out_shape=[jax.ShapeDtypeStruct((t, nj), jnp.int32),
                   jax.ShapeDtypeStruct((t, nj), jnp.int32),
                   jax.ShapeDtypeStruct((t, nj), F32)],
        scratch_shapes=[slabs(PEER_NKEYS, F32), slabs(PEER_NKEYS, jnp.int32),
                        slabs(PEER_TOPK, F32), slabs(PEER_TOPK, jnp.int32),
                        slabs(PEER_TOPK, F32), slabs(PEER_TOPK, jnp.int32),
                        slabs(len(_CAND), F32), slabs(len(_CAND), jnp.int32),
                        slabs(PEER_TOPK, F32), slabs(PEER_TOPK, jnp.int32),
                        slabs(PEER_TOPK, jnp.int32), slabs(PEER_TOPK, jnp.int32)],
        compiler_params=_cparams(("parallel",)),
        name="peer_query",
    )(hb, wq, keys)


BUILD_UNROLL = 16


def _gelu(x):
    return 0.5 * x * (1.0 + lax.erf(x * (1.0 / math.sqrt(2.0))))


def _pack_bf16_pair(a, b):
    bits = lambda v: lax.bitcast_convert_type(v.astype(BF16).astype(F32), jnp.uint32)
    return bits(a) | (bits(b) >> 16)


def _unpack_bf16_pair(w):
    return (lax.bitcast_convert_type(w & jnp.uint32(0xFFFF0000), F32),
            lax.bitcast_convert_type(w << 16, F32))


def _route_rows(i1_row, i2_row, gate_row, key_iota):
    a_t = jnp.where(key_iota == i1_row, gate_row, 0.0).astype(BF16)
    b_t = jnp.where(key_iota == i2_row, 1.0, 0.0).astype(BF16)
    return lax.dot_general(a_t, b_t, (((1,), (1,)), ((), ())), preferred_element_type=F32)


def _peer_expert_kernel(hb_ref, h_ref, i1_ref, i2_ref, gate_ref, u_ref, v_ref, g_ref, b_ref,
                        o_ref, gs_ref, *, tm, te, stride):
    e = pl.program_id(1)
    half = tm // 2

    @pl.when(e == 0)
    def _():
        o_ref[...] = jnp.zeros(o_ref.shape, F32)
        key_iota = lax.broadcasted_iota(jnp.int32, (PEER_NKEYS, PEER_NKEYS), 0)

        def build(blk, carry):
            lo = pl.multiple_of(blk * BUILD_UNROLL, BUILD_UNROLL)
            hi = pl.multiple_of(half + blk * BUILD_UNROLL, BUILD_UNROLL)
            rows = [(r[pl.ds(lo, BUILD_UNROLL), :], r[pl.ds(hi, BUILD_UNROLL), :])
                    for r in (i1_ref, i2_ref, gate_ref)]
            for k in range(BUILD_UNROLL):
                g0 = _route_rows(rows[0][0][k:k + 1], rows[1][0][k:k + 1], rows[2][0][k:k + 1], key_iota)
                g1 = _route_rows(rows[0][1][k:k + 1], rows[1][1][k:k + 1], rows[2][1][k:k + 1], key_iota)
                gs_ref[pl.ds(lo + k, PEER_NKEYS, stride=stride), :] = _pack_bf16_pair(g0, g1)
            return carry

        lax.fori_loop(0, half // BUILD_UNROLL, build, 0)

    s = lax.dot_general(hb_ref[...], u_ref[...], (((1,), (1,)), ((), ())),
                        preferred_element_type=F32)
    slabs = te // PEER_NKEYS
    packed = jnp.concatenate(
        [gs_ref[pl.ds(pl.multiple_of((e * slabs + r) * stride, SUBLANES), half), :]
         for r in range(slabs)], axis=1)
    route = jnp.concatenate(_unpack_bf16_pair(packed), axis=0)
    act = (_gelu(s) * route).astype(BF16)
    o_ref[...] += jnp.dot(act, v_ref[...], preferred_element_type=F32)

    @pl.when(e == pl.num_programs(1) - 1)
    def _():
        o_ref[...] = _layer_norm(DN_ALPHA * h_ref[...] + o_ref[...], g_ref[...], b_ref[...])


def _peer_expert_call(hb, h, i1, i2, gate, u, v, g, b, tm, te):
    t = hb.shape[0]
    nj = PEER_TOPK * PEER_HEADS
    assert tm % (2 * BUILD_UNROLL) == 0
    stride = tm // 2 + SUBLANES
    rmap = lambda i, e: (i, 0)
    emap = lambda i, e: (e, 0)
    cmap = lambda i, e: (0, 0)
    once = dict(pipeline_mode=pl.Buffered(1))
    return pl.pallas_call(
        functools.partial(_peer_expert_kernel, tm=tm, te=te, stride=stride),
        grid=(t // tm, PEER_EXPERTS // te),
        in_specs=[pl.BlockSpec((tm, D_MODEL), rmap),
                  pl.BlockSpec((tm, D_MODEL), rmap, **once),
                  pl.BlockSpec((tm, nj), rmap),
                  pl.BlockSpec((tm, nj), rmap),
                  pl.BlockSpec((tm, nj), rmap),
                  pl.BlockSpec((te, D_MODEL), emap),
                  pl.BlockSpec((te, D_MODEL), emap),
                  pl.BlockSpec((1, D_MODEL), cmap),
                  pl.BlockSpec((1, D_MODEL), cmap)],
        out_specs=pl.BlockSpec((tm, D_MODEL), rmap),
        out_shape=jax.ShapeDtypeStruct((t, D_MODEL), F32),
        scratch_shapes=[pltpu.VMEM((PEER_NKEYS * stride, PEER_NKEYS), jnp.uint32)],
        compiler_params=_cparams(("parallel", "arbitrary"), PEER_VMEM_LIMIT),
        name="peer_experts",
    )(hb, h, i1, i2, gate, u, v, g, b)


def kernel(x, meta, ln0_g, ln0_b, w_in, ssd_conv_w, ssd_conv_b, ssd_dt_bias, ssd_a_log, ssd_d,
           ssd_norm_w, ssd_out, cf_dw_w, cf_dw_b, cf_ln_g, cf_ln_b, cf_out, w_o, ln1_g, ln1_b,
           peer_wq, peer_keys, peer_u, peer_v, ln2_g, ln2_b):
    bsz, seq, d = x.shape
    assert d == D_MODEL and seq % CHUNK == 0
    assert w_in.shape[0] == 1, "single layer"
    t = bsz * seq
    tp = t + bsz * CHUNK
    ln_rows = _divisor_tile(t, 512, 2 * CHUNK)
    assert bsz * CHUNK <= ln_rows

    rows = _divisor_tile(seq, 256, CHUNK)
    tm_proj = _divisor_tile(tp, 1664, SUBLANES)
    tn_proj = 1024
    tm_merge = _divisor_tile(t, 256, SUBLANES)
    tq = _divisor_tile(t, 512, LANES)
    tm_peer = _divisor_tile(t, 512, 2 * BUILD_UNROLL)

    h0, h0b = _ln0_call(x.reshape(t, d), meta, ln0_g, ln0_b, tp, ln_rows)

    w = w_in[0].astype(BF16)
    wcv = w[:, OFF_DT:OFF_DT + CF_WIDTH]
    wcg = w[:, OFF_DT + CF_WIDTH:OFF_CF]
    wgate = w[:, OFF_CF:]
    pad_heads = lambda a: jnp.pad(a.astype(F32), (0, LANES - SSD_HEADS)).reshape(1, LANES)
    head_expand = jnp.tile(jnp.arange(SSD_D_INNER)[None, :] // SSD_HEADDIM
                           == jnp.arange(LANES)[:, None], (3, 1)).astype(BF16)

    zs = _proj_call(h0b, [w], _silu, BF16, tm_proj, tn_proj, name="proj_z", window=(0, OFF_Z))
    xbc = _proj_call(h0b, [w], lambda a: a, BF16, tm_proj, tn_proj, name="proj_xbc",
                     window=(OFF_Z, SSD_XBC))
    dt = _proj_call(h0b, [w], _softplus, F32, tm_proj, LANES, bias=pad_heads(ssd_dt_bias[0]),
                    name="proj_dt", window=(OFF_XBC, LANES))
    gates = _proj_call(h0b, [wgate], jax.nn.sigmoid, BF16, tm_proj, tn_proj, name="proj_gate")

    yn = _ssd_call(xbc, zs, dt, ssd_conv_w[0], ssd_conv_b[0].reshape(1, -1), pad_heads(ssd_a_log[0]),
                   jnp.repeat(ssd_d[0].astype(F32), SSD_HEADDIM).reshape(1, -1),
                   ssd_norm_w[0].reshape(1, -1), head_expand, bsz, seq, rows)
    c2 = _conf_call(h0b, wcv, wcg, cf_dw_w[0], cf_dw_b[0].reshape(1, -1), cf_ln_g[0].reshape(1, -1),
                    cf_ln_b[0].reshape(1, -1), bsz, seq, rows)
    h1, h1b = _merge_call(yn, c2, gates, h0, ssd_out[0].astype(BF16), cf_out[0].astype(BF16),
                          w_o[0].astype(BF16), ln1_g[0].reshape(1, -1), ln1_b[0].reshape(1, -1),
                          tm_merge)

    i1, i2, gate = _peer_query_call(h1b, peer_wq[0].astype(BF16), peer_keys[0].astype(BF16), tq)
    out = _peer_expert_call(h1b, h1, i1, i2, gate, peer_u[0].astype(BF16), peer_v[0].astype(BF16),
                            ln2_g[0].reshape(1, -1), ln2_b[0].reshape(1, -1), tm_peer, 1024)
    return out.reshape(bsz, seq, d)
```

```python
import functools
import math

import jax
import jax.numpy as jnp
from jax import lax
from jax.experimental import pallas as pl
from jax.experimental.pallas import tpu as pltpu

F32 = jnp.float32
BF16 = jnp.bfloat16

D_MODEL = 2048
CHUNK = 64
N_META = 16
PAD_ROWS = CHUNK - N_META
SSD_D_INNER = 2048
SSD_HEADDIM = 64
SSD_HEADS = 32
SSD_GROUPS = 8
SSD_STATE = 128
SSD_CONV = 4
SSD_XBC = SSD_D_INNER + 2 * SSD_GROUPS * SSD_STATE
GROUP_COLS = SSD_D_INNER // SSD_GROUPS
HEADS_PER_GROUP = SSD_HEADS // SSD_GROUPS
CF_WIDTH = 2048
CF_CONV = 31
CF_HALO = 32
PEER_HEADS = 8
PEER_NKEYS = 128
PEER_EXPERTS = PEER_NKEYS * PEER_NKEYS
PEER_DKEY = 256
PEER_TOPK = 16
OFF_Z = SSD_D_INNER
OFF_XBC = OFF_Z + SSD_XBC
OFF_DT = OFF_XBC + SSD_HEADS
OFF_CF = OFF_DT + 2 * CF_WIDTH
LN_EPS = 1e-5
DN_ALPHA = 2.0 ** 0.25
LANES = 128
SUBLANES = 8
VMEM_LIMIT = 56 * 1024 * 1024
PEER_VMEM_LIMIT = 60 * 1024 * 1024

_CAND = [(a, b) for a in range(PEER_TOPK) for b in range(PEER_TOPK)
         if (a + 1) * (b + 1) <= PEER_TOPK]


def _cparams(sem, vmem_limit=VMEM_LIMIT):
    return pltpu.CompilerParams(dimension_semantics=sem, vmem_limit_bytes=vmem_limit)


def _resident(shape):
    nd = len(shape)
    return pl.BlockSpec(shape, lambda *_: (0,) * nd, pipeline_mode=pl.Buffered(1))


def _divisor_tile(n, target, mult):
    best = None
    for t in range(mult, min(n, target) + 1, mult):
        if n % t == 0:
            best = t
    assert best is not None, (n, target, mult)
    return best


def _layer_norm(x, g, b):
    mu = jnp.mean(x, axis=-1, keepdims=True)
    xc = x - mu
    var = jnp.mean(xc * xc, axis=-1, keepdims=True)
    return xc * lax.rsqrt(var + LN_EPS) * g + b


def _silu(x):
    return x * jax.nn.sigmoid(x)


def _ln0_kernel(x_ref, meta_ref, g_ref, b_ref, o_ref, ob_ref, *, n_token_tiles):
    i = pl.program_id(0)

    def emit(rows):
        y = _layer_norm(rows, g_ref[...], b_ref[...])
        o_ref[...] = y
        ob_ref[...] = y.astype(BF16)

    @pl.when(i < n_token_tiles)
    def _():
        emit(x_ref[...])

    @pl.when(i >= n_token_tiles)
    def _():
        head = jnp.concatenate([jnp.zeros((PAD_ROWS, D_MODEL), F32), meta_ref[...]], axis=0)
        emit(jnp.concatenate([head] * (x_ref.shape[0] // CHUNK), axis=0))


def _ln0_call(x2d, meta, g, b, tp, rows):
    t, d = x2d.shape
    n_tok = t // rows
    return pl.pallas_call(
        functools.partial(_ln0_kernel, n_token_tiles=n_tok),
        grid=(pl.cdiv(tp, rows),),
        in_specs=[pl.BlockSpec((rows, d), lambda i: (jnp.minimum(i, n_tok - 1), 0)),
                  pl.BlockSpec((N_META, d), lambda i: (0, 0)),
                  pl.BlockSpec((1, d), lambda i: (0, 0)),
                  pl.BlockSpec((1, d), lambda i: (0, 0))],
        out_specs=[pl.BlockSpec((rows, d), lambda i: (i, 0)),
                   pl.BlockSpec((rows, d), lambda i: (i, 0))],
        out_shape=[jax.ShapeDtypeStruct((tp, d), F32), jax.ShapeDtypeStruct((tp, d), BF16)],
        compiler_params=_cparams(("parallel",)),
        name="ln0",
    )(x2d, meta, g.reshape(1, d), b.reshape(1, d))


def _proj_kernel(x_ref, *refs, n_w, epilogue, has_bias):
    w_refs = refs[:n_w]
    b_ref = refs[n_w] if has_bias else None
    o_ref = refs[-1]
    x = x_ref[...]
    accs = [jnp.dot(x, w[...], preferred_element_type=F32) for w in w_refs]
    if has_bias:
        accs[0] = accs[0] + b_ref[...]
    o_ref[...] = epilogue(*accs).astype(o_ref.dtype)


def _proj_call(x, ws, epilogue, out_dtype, tm, tn, bias=None, name="proj", window=None):
    tp, k = x.shape
    col0, n = (0, ws[0].shape[1]) if window is None else window
    assert col0 % tn == 0 and n % tn == 0
    blk0 = col0 // tn
    in_specs = [pl.BlockSpec((tm, k), lambda i, j: (i, 0))]
    in_specs += [pl.BlockSpec((k, tn), lambda i, j: (0, blk0 + j)) for _ in ws]
    args = [x, *ws]
    if bias is not None:
        in_specs.append(pl.BlockSpec((1, tn), lambda i, j: (0, j)))
        args.append(bias)
    return pl.pallas_call(
        functools.partial(_proj_kernel, n_w=len(ws), epilogue=epilogue, has_bias=bias is not None),
        grid=(tp // tm, n // tn),
        in_specs=in_specs,
        out_specs=pl.BlockSpec((tm, tn), lambda i, j: (i, j)),
        out_shape=jax.ShapeDtypeStruct((tp, n), out_dtype),
        compiler_params=_cparams(("parallel", "parallel")),
        name=name,
    )(*args)


def _softplus(x):
    return jnp.maximum(x, 0.0) + jnp.log1p(jnp.exp(-jnp.abs(x)))


def _split3(x):
    hi = x.astype(BF16)
    r1 = x - hi.astype(F32)
    mid = r1.astype(BF16)
    lo = (r1 - mid.astype(F32)).astype(BF16)
    return jnp.concatenate([hi, mid, lo], axis=1)


def _ssd_block(xin, dt, n, refs, y_ref):
    cw_ref, cb_ref, alog_ref, hexp_ref, ext_ref, state_ref = refs
    hist = SUBLANES
    ext_ref[hist:hist + n, :] = xin
    acc = cb_ref[...] + cw_ref[0:1, :] * ext_ref[pl.ds(hist - (SSD_CONV - 1), n), :]
    for k in range(1, SSD_CONV):
        acc = acc + cw_ref[k:k + 1, :] * ext_ref[pl.ds(hist - (SSD_CONV - 1) + k, n), :]
    ext_ref[0:hist, :] = ext_ref[n:n + hist, :]
    xc = _silu(acc)
    xs = xc[:, :SSD_D_INNER]
    bm = xc[:, SSD_D_INNER:SSD_D_INNER + SSD_GROUPS * SSD_STATE].astype(BF16)
    cm = xc[:, SSD_D_INNER + SSD_GROUPS * SSD_STATE:].astype(BF16)

    da = dt * (-jnp.exp(alog_ref[...]))
    pos = lax.broadcasted_iota(jnp.int32, (n, 1), 0) % CHUNK
    cs = da
    shift = 1
    while shift < CHUNK:
        cs = cs + jnp.where(pos >= shift, pltpu.roll(cs, shift, axis=0), 0.0)
        shift *= 2
    hexp = hexp_ref[...]
    dt_e = jnp.dot(_split3(dt), hexp, preferred_element_type=F32)
    cs_e = jnp.dot(_split3(cs), hexp, preferred_element_type=F32)
    xdt = xs * dt_e

    sub = lax.broadcasted_iota(jnp.int32, (CHUNK, SSD_D_INNER), 0)
    lane_pos = lax.broadcasted_iota(jnp.int32, (CHUNK, SSD_D_INNER), 1) % SSD_HEADDIM
    diag = sub == lane_pos
    causal = sub >= lane_pos
    blk_r = lax.broadcasted_iota(jnp.int32, (GROUP_COLS, GROUP_COLS), 0) // SSD_HEADDIM
    blk_c = lax.broadcasted_iota(jnp.int32, (GROUP_COLS, GROUP_COLS), 1) // SSD_HEADDIM
    same_head = blk_r == blk_c

    for c in range(n // CHUNK):
        sl = slice(c * CHUNK, (c + 1) * CHUNK)
        cs_c = cs_e[sl]
        cs_end = cs_c[CHUNK - 1:CHUNK, :]
        decay_out = jnp.exp(cs_end - cs_c)
        decay_chunk = jnp.exp(cs_end)
        if y_ref is not None:
            cs_row = jnp.sum(jnp.where(diag, cs_c, 0.0), axis=0, keepdims=True)
            decay_l = jnp.where(causal, jnp.exp(cs_c - cs_row), 0.0)
            decay_in = jnp.exp(cs_c)
        for g in range(SSD_GROUPS):
            gc = slice(g * GROUP_COLS, (g + 1) * GROUP_COLS)
            gn = slice(g * SSD_STATE, (g + 1) * SSD_STATE)
            bm_g = bm[sl, gn]
            x_g = xdt[sl, gc]
            st = state_ref[g]
            if y_ref is not None:
                cm_g = cm[sl, gn]
                bm_rep = jnp.concatenate([bm_g] * HEADS_PER_GROUP, axis=0)
                scores = lax.dot_general(cm_g, bm_rep, (((1,), (1,)), ((), ())),
                                         preferred_element_type=F32)
                m = (scores * decay_l[:, gc]).astype(BF16)
                x_rep = jnp.concatenate([x_g] * HEADS_PER_GROUP, axis=0)
                x_bd = jnp.where(same_head, x_rep, 0.0).astype(BF16)
                y_diag = jnp.dot(m, x_bd, preferred_element_type=F32)
                y_off = jnp.dot(cm_g, st.astype(BF16), preferred_element_type=F32) * decay_in[:, gc]
                y_ref[sl, gc] = y_diag + y_off
            xd = (x_g * decay_out[:, gc]).astype(BF16)
            upd = lax.dot_general(bm_g, xd, (((0,), (0,)), ((), ())), preferred_element_type=F32)
            state_ref[g] = st * decay_chunk[:, gc] + upd
    return xs


def _ssd_kernel(xbc_ref, zs_ref, dt_ref, xbc_head_ref, dt_head_ref, cw_ref, cb_ref, alog_ref,
                dskip_ref, nw_ref, hexp_ref, o_ref, ext_ref, state_ref, y_ref, *, rows):
    refs = (cw_ref, cb_ref, alog_ref, hexp_ref, ext_ref, state_ref)

    @pl.when(pl.program_id(1) == 0)
    def _():
        ext_ref[0:SUBLANES, :] = jnp.zeros((SUBLANES, SSD_XBC), F32)
        state_ref[...] = jnp.zeros(state_ref.shape, F32)
        meta_row = lax.broadcasted_iota(jnp.int32, (CHUNK, 1), 0) >= PAD_ROWS
        _ssd_block(jnp.where(meta_row, xbc_head_ref[...].astype(F32), 0.0),
                   jnp.where(meta_row, dt_head_ref[...], 0.0), CHUNK, refs, None)

    xs = _ssd_block(xbc_ref[...].astype(F32), dt_ref[...], rows, refs, y_ref)
    y = y_ref[...] + dskip_ref[...] * xs
    yf = y * zs_ref[...].astype(F32)
    for g in range(SSD_GROUPS):
        gc = slice(g * GROUP_COLS, (g + 1) * GROUP_COLS)
        seg = yf[:, gc]
        ms = jnp.mean(seg * seg, axis=-1, keepdims=True)
        o_ref[:, gc] = (seg * lax.rsqrt(ms + LN_EPS) * nw_ref[:, gc]).astype(o_ref.dtype)


def _ssd_call(xbc, zs, dt, cw, cb, alog, dskip, nw, hexp, bsz, seq, rows):
    nblk = seq // rows
    head0 = bsz * seq // CHUNK
    rmap = lambda b, i: (b * nblk + i, 0)
    hmap = lambda b, i: (head0 + b, 0)
    cmap = lambda b, i: (0, 0)
    return pl.pallas_call(
        functools.partial(_ssd_kernel, rows=rows),
        grid=(bsz, nblk),
        in_specs=[pl.BlockSpec((rows, SSD_XBC), rmap),
                  pl.BlockSpec((rows, SSD_D_INNER), rmap),
                  pl.BlockSpec((rows, LANES), rmap),
                  pl.BlockSpec((CHUNK, SSD_XBC), hmap),
                  pl.BlockSpec((CHUNK, LANES), hmap),
                  pl.BlockSpec((SSD_CONV, SSD_XBC), cmap),
                  pl.BlockSpec((1, SSD_XBC), cmap),
                  pl.BlockSpec((1, LANES), cmap),
                  pl.BlockSpec((1, SSD_D_INNER), cmap),
                  pl.BlockSpec((1, SSD_D_INNER), cmap),
                  pl.BlockSpec((3 * LANES, SSD_D_INNER), cmap)],
        out_specs=pl.BlockSpec((rows, SSD_D_INNER), rmap),
        out_shape=jax.ShapeDtypeStruct((bsz * seq, SSD_D_INNER), BF16),
        scratch_shapes=[pltpu.VMEM((SUBLANES + rows, SSD_XBC), F32),
                        pltpu.VMEM((SSD_GROUPS, SSD_STATE, GROUP_COLS), F32),
                        pltpu.VMEM((rows, SSD_D_INNER), F32)],
        compiler_params=_cparams(("arbitrary", "arbitrary")),
        name="ssd",
    )(xbc, zs, dt, xbc, dt, cw, cb, alog, dskip, nw, hexp)


CF_ROW_BLK = 64
CF_COL_BLK = 256


def _glu_proj(x, wv_ref, wg_ref, cols):
    val = jnp.dot(x, wv_ref[:, cols], preferred_element_type=F32)
    gate = jnp.dot(x, wg_ref[:, cols], preferred_element_type=F32)
    return val * jax.nn.sigmoid(gate)


def _conf_kernel(x0_ref, xnext_ref, xhead_ref, wv_ref, wg_ref, w_ref, b_ref, g_ref, beta_ref, o_ref,
                 ext_ref, sh_ref, acc_ref, cnext_ref, *, rows):
    col_blocks = CF_WIDTH // CF_COL_BLK

    @pl.when(pl.program_id(1) == 0)
    def _():
        meta_row = lax.broadcasted_iota(jnp.int32, (CF_HALO, 1), 0) >= CF_HALO - N_META
        x_tail = xhead_ref[CHUNK - CF_HALO:CHUNK, :]
        for cb in range(col_blocks):
            cols = slice(cb * CF_COL_BLK, (cb + 1) * CF_COL_BLK)
            ext_ref[0:CF_HALO, cols] = jnp.where(meta_row, _glu_proj(x_tail, wv_ref, wg_ref, cols), 0.0)
            cnext_ref[:, cols] = _glu_proj(x0_ref[...], wv_ref, wg_ref, cols)

    ext_ref[CF_HALO:CF_HALO + rows, :] = cnext_ref[...]
    span = rows + CF_HALO - SUBLANES
    for s in range(1, SUBLANES):
        sh_ref[s - 1] = ext_ref[pl.ds(s, span), :]
    base = CF_HALO - (CF_CONV - 1)

    def col_block(cb, carry):
        cols = pl.ds(pl.multiple_of(cb * CF_COL_BLK, CF_COL_BLK), CF_COL_BLK)
        nxt = _glu_proj(xnext_ref[...], wv_ref, wg_ref, cols)
        for rb in range(rows // CF_ROW_BLK):
            r0 = rb * CF_ROW_BLK
            acc = jnp.broadcast_to(b_ref[:, cols], (CF_ROW_BLK, CF_COL_BLK))
            for k in range(CF_CONV):
                s = (base + k) % SUBLANES
                q = r0 + (base + k - s)
                if s == 0:
                    src = ext_ref[q:q + CF_ROW_BLK, cols]
                else:
                    src = sh_ref[s - 1, q:q + CF_ROW_BLK, cols]
                acc = acc + w_ref[k:k + 1, cols] * src
            acc_ref[r0:r0 + CF_ROW_BLK, cols] = acc
        cnext_ref[:, cols] = nxt
        return carry

    lax.fori_loop(0, col_blocks, col_block, 0)
    ext_ref[0:CF_HALO, :] = ext_ref[rows:rows + CF_HALO, :]
    o_ref[...] = _silu(_layer_norm(acc_ref[...], g_ref[...], beta_ref[...])).astype(o_ref.dtype)


def _conf_call(hb, wv, wg, w, b, g, beta, bsz, seq, rows):
    nblk = seq // rows
    head0 = bsz * seq // CHUNK
    rmap = lambda bb, i: (bb * nblk + i, 0)
    first = lambda bb, i: (bb * nblk, 0)
    nxt = lambda bb, i: (bb * nblk + jnp.minimum(i + 1, nblk - 1), 0)
    hmap = lambda bb, i: (head0 + bb, 0)
    cmap = lambda bb, i: (0, 0)
    return pl.pallas_call(
        functools.partial(_conf_kernel, rows=rows),
        grid=(bsz, nblk),
        in_specs=[pl.BlockSpec((rows, D_MODEL), first),
                  pl.BlockSpec((rows, D_MODEL), nxt),
                  pl.BlockSpec((CHUNK, D_MODEL), hmap),
                  _resident((D_MODEL, CF_WIDTH)),
                  _resident((D_MODEL, CF_WIDTH)),
                  pl.BlockSpec((CF_CONV, CF_WIDTH), cmap),
                  pl.BlockSpec((1, CF_WIDTH), cmap),
                  pl.BlockSpec((1, CF_WIDTH), cmap),
                  pl.BlockSpec((1, CF_WIDTH), cmap)],
        out_specs=pl.BlockSpec((rows, CF_WIDTH), rmap),
        out_shape=jax.ShapeDtypeStruct((bsz * seq, CF_WIDTH), BF16),
        scratch_shapes=[pltpu.VMEM((CF_HALO + rows, CF_WIDTH), F32),
                        pltpu.VMEM((SUBLANES - 1, CF_HALO + rows - SUBLANES, CF_WIDTH), F32),
                        pltpu.VMEM((rows, CF_WIDTH), F32),
                        pltpu.VMEM((rows, CF_WIDTH), F32)],
        compiler_params=_cparams(("arbitrary", "arbitrary")),
        name="conformer",
    )(hb, hb, hb, wv, wg, w, b, g, beta)


def _merge_kernel(yn_ref, c_ref, gate_ref, h_ref, wssd_ref, wcf_ref, wo_ref, g_ref, b_ref,
                  o_ref, ob_ref):
    y_ssd = jnp.dot(yn_ref[...], wssd_ref[...], preferred_element_type=F32)
    y_cf = jnp.dot(c_ref[...], wcf_ref[...], preferred_element_type=F32)
    gates = gate_ref[...].astype(F32)
    mix = gates[:, :D_MODEL] * y_ssd + gates[:, D_MODEL:] * y_cf
    m = jnp.dot(mix.astype(BF16), wo_ref[...], preferred_element_type=F32)
    h1 = _layer_norm(DN_ALPHA * h_ref[...] + m, g_ref[...], b_ref[...])
    o_ref[...] = h1
    ob_ref[...] = h1.astype(BF16)


def _merge_call(yn, c, gates, h0, wssd, wcf, wo, g, b, tm):
    t = yn.shape[0]
    rmap = lambda i: (i, 0)
    return pl.pallas_call(
        _merge_kernel,
        grid=(t // tm,),
        in_specs=[pl.BlockSpec((tm, SSD_D_INNER), rmap),
                  pl.BlockSpec((tm, CF_WIDTH), rmap),
                  pl.BlockSpec((tm, 2 * D_MODEL), rmap),
                  pl.BlockSpec((tm, D_MODEL), rmap),
                  _resident((SSD_D_INNER, D_MODEL)),
                  _resident((CF_WIDTH, D_MODEL)),
                  _resident((D_MODEL, D_MODEL)),
                  _resident((1, D_MODEL)),
                  _resident((1, D_MODEL))],
        out_specs=[pl.BlockSpec((tm, D_MODEL), rmap), pl.BlockSpec((tm, D_MODEL), rmap)],
        out_shape=[jax.ShapeDtypeStruct((t, D_MODEL), F32),
                   jax.ShapeDtypeStruct((t, D_MODEL), BF16)],
        compiler_params=_cparams(("parallel",)),
        name="merge",
    )(yn, c, gates, h0, wssd, wcf, wo, g, b)


def _slab(ref, j):
    if isinstance(j, int):
        return ref.at[:, j * PEER_HEADS:(j + 1) * PEER_HEADS, :]
    return ref.at[:, pl.ds(pl.multiple_of(j * PEER_HEADS, PEER_HEADS), PEER_HEADS), :]


def _top16_major(s_ref, ix_ref, val_ref, idx_ref, n, ids=None):
    ids = list(range(n)) if ids is None else ids
    assert n % 2 == 0 and all(x < y for x, y in zip(ids, ids[1:]))
    none = ids[-1] + 1
    neg = jnp.float32(-jnp.inf)
    half = n // 2
    m0 = None
    for p in range(half):
        a = _slab(s_ref, p)[...]
        b = _slab(s_ref, p + half)[...]
        swap = b > a
        front = jnp.where(swap, b, a)
        _slab(s_ref, p)[...] = front
        _slab(s_ref, p + half)[...] = jnp.where(swap, a, b)
        _slab(ix_ref, p)[...] = jnp.where(swap, ids[p + half], ids[p])
        _slab(ix_ref, p + half)[...] = jnp.where(swap, ids[p], ids[p + half])
        m0 = front if m0 is None else jnp.maximum(m0, front)

    def body(it, m):
        sel = jnp.full(m.shape, none, jnp.int32)
        for p in range(half):
            sel = jnp.minimum(sel, jnp.where(_slab(s_ref, p)[...] == m, _slab(ix_ref, p)[...], none))
        _slab(val_ref, it)[...] = m
        _slab(idx_ref, it)[...] = sel
        nxt = jnp.full(m.shape, neg, F32)
        for p in range(half):
            hit = _slab(ix_ref, p)[...] == sel
            back = _slab(s_ref, p + half)[...]
            front = jnp.where(hit, back, _slab(s_ref, p)[...])
            _slab(s_ref, p)[...] = front
            _slab(s_ref, p + half)[...] = jnp.where(hit, neg, back)
            _slab(ix_ref, p)[...] = jnp.where(hit, _slab(ix_ref, p + half)[...], _slab(ix_ref, p)[...])
            nxt = jnp.maximum(nxt, front)
        return nxt

    lax.fori_loop(0, PEER_TOPK, body, m0)


def _peer_query_kernel(hb_ref, wq_ref, keys_ref, i1_ref, i2_ref, gate_ref,
                       s_ref, ix_ref, v1_ref, x1_ref, v2_ref, x2_ref, cand_ref, cix_ref, bs_ref, sel_ref,
                       e1_ref, e2_ref):
    tq = hb_ref.shape[0]
    lane_blocks = tq // LANES
    q = jnp.dot(hb_ref[...], wq_ref[...], preferred_element_type=F32).astype(BF16)
    half_cols = PEER_DKEY // 2
    for half, (v_ref, x_ref) in enumerate(((v1_ref, x1_ref), (v2_ref, x2_ref))):
        for h in range(PEER_HEADS):
            c0 = h * PEER_DKEY + half * half_cols
            scores = lax.dot_general(keys_ref[half, h], q[:, c0:c0 + half_cols],
                                     (((1,), (1,)), ((), ())), preferred_element_type=F32)
            for lb in range(lane_blocks):
                s_ref[lb, pl.ds(h, PEER_NKEYS, stride=PEER_HEADS), :] = (
                    scores[:, lb * LANES:(lb + 1) * LANES])
        _top16_major(s_ref, ix_ref, v_ref, x_ref, PEER_NKEYS)

    for ci, (a, b) in enumerate(_CAND):
        _slab(cand_ref, ci)[...] = _slab(v1_ref, a)[...] + _slab(v2_ref, b)[...]
    _top16_major(cand_ref, cix_ref, bs_ref, sel_ref, len(_CAND),
                 ids=[a * PEER_TOPK + b for a, b in _CAND])

    def key_ids(it, carry):
        flat = _slab(sel_ref, it)[...]
        k1 = lax.shift_right_logical(flat, int(math.log2(PEER_TOPK)))
        k2 = flat & (PEER_TOPK - 1)
        e1 = jnp.zeros(flat.shape, jnp.int32)
        e2 = jnp.zeros(flat.shape, jnp.int32)
        for r in range(PEER_TOPK):
            e1 = jnp.where(k1 == r, _slab(x1_ref, r)[...], e1)
            e2 = jnp.where(k2 == r, _slab(x2_ref, r)[...], e2)
        _slab(e1_ref, it)[...] = e1
        _slab(e2_ref, it)[...] = e2
        return carry

    lax.fori_loop(0, PEER_TOPK, key_ids, 0)

    for lb in range(lane_blocks):
        bs = bs_ref[lb].reshape(PEER_TOPK, PEER_HEADS, LANES)
        ex = jnp.exp(bs - bs[0:1])
        gate = ex / jnp.sum(ex, axis=0, keepdims=True)
        tok = slice(lb * LANES, (lb + 1) * LANES)
        gate_ref[tok, :] = gate.reshape(PEER_TOPK * PEER_HEADS, LANES).T
        i1_ref[tok, :] = e1_ref[lb].T
        i2_ref[tok, :] = e2_ref[lb].T


def _peer_query_call(hb, wq, keys, tq):
    t = hb.shape[0]
    nj = PEER_TOPK * PEER_HEADS
    rmap = lambda i: (i, 0)
    slabs = lambda n, dt: pltpu.VMEM((tq // LANES, n * PEER_HEADS, LANES), dt)
    return pl.pallas_call(
        _peer_query_kernel,
        grid=(t // tq,),
        in_specs=[pl.BlockSpec((tq, D_MODEL), rmap),
                  _resident((D_MODEL, PEER_HEADS * PEER_DKEY)),
                  _resident((2, PEER_HEADS, PEER_NKEYS, PEER_DKEY // 2))],
        out_specs=[pl.BlockSpec((tq, nj), rmap)] * 3,
        out_shape=[jax.ShapeDtypeStruct((t, nj), jnp.int32),
                   jax.ShapeDtypeStruct((t, nj), jnp.int32),
                   jax.ShapeDtypeStruct((t, nj), F32)],
        scratch_shapes=[slabs(PEER_NKEYS, F32), slabs(PEER_NKEYS, jnp.int32),
                        slabs(PEER_TOPK, F32), slabs(PEER_TOPK, jnp.int32),
                        slabs(PEER_TOPK, F32), slabs(PEER_TOPK, jnp.int32),
                        slabs(len(_CAND), F32), slabs(len(_CAND), jnp.int32),
                        slabs(PEER_TOPK, F32), slabs(PEER_TOPK, jnp.int32),
                        slabs(PEER_TOPK, jnp.int32), slabs(PEER_TOPK, jnp.int32)],
        compiler_params=_cparams(("parallel",)),
        name="peer_query",
    )(hb, wq, keys)


BUILD_UNROLL = 32


def _gelu(x):
    return 0.5 * x * (1.0 + lax.erf(x * (1.0 / math.sqrt(2.0))))


def _pack_bf16_pair(a, b):
    bits = lambda v: lax.bitcast_convert_type(v.astype(BF16).astype(F32), jnp.uint32)
    return bits(a) | (bits(b) >> 16)


def _unpack_bf16_pair(w):
    return (lax.bitcast_convert_type(w & jnp.uint32(0xFFFF0000), F32),
            lax.bitcast_convert_type(w << 16, F32))


def _route_rows(i1_row, i2_row, gate_row, key_iota):
    a_t = jnp.where(key_iota == i1_row, gate_row, 0.0).astype(BF16)
    b_t = jnp.where(key_iota == i2_row, 1.0, 0.0).astype(BF16)
    return lax.dot_general(a_t, b_t, (((1,), (1,)), ((), ())), preferred_element_type=F32)


def _peer_expert_kernel(hb_ref, h_ref, i1_ref, i2_ref, gate_ref, u_ref, v_ref, g_ref, b_ref,
                        o_ref, gs_ref, *, tm, te, stride):
    e = pl.program_id(1)
    half = tm // 2

    @pl.when(e == 0)
    def _():
        o_ref[...] = jnp.zeros(o_ref.shape, F32)
        key_iota = lax.broadcasted_iota(jnp.int32, (PEER_NKEYS, PEER_NKEYS), 0)

        def build(blk, carry):
            lo = pl.multiple_of(blk * BUILD_UNROLL, BUILD_UNROLL)
            hi = pl.multiple_of(half + blk * BUILD_UNROLL, BUILD_UNROLL)
            rows = [(r[pl.ds(lo, BUILD_UNROLL), :], r[pl.ds(hi, BUILD_UNROLL), :])
                    for r in (i1_ref, i2_ref, gate_ref)]
            for k in range(BUILD_UNROLL):
                g0 = _route_rows(rows[0][0][k:k + 1], rows[1][0][k:k + 1], rows[2][0][k:k + 1], key_iota)
                g1 = _route_rows(rows[0][1][k:k + 1], rows[1][1][k:k + 1], rows[2][1][k:k + 1], key_iota)
                gs_ref[pl.ds(lo + k, PEER_NKEYS, stride=stride), :] = _pack_bf16_pair(g0, g1)
            return carry

        lax.fori_loop(0, half // BUILD_UNROLL, build, 0)

    s = lax.dot_general(hb_ref[...], u_ref[...], (((1,), (1,)), ((), ())),
                        preferred_element_type=F32)
    slabs = te // PEER_NKEYS
    packed = jnp.concatenate(
        [gs_ref[pl.ds(pl.multiple_of((e * slabs + r) * stride, SUBLANES), half), :]
         for r in range(slabs)], axis=1)
    route = jnp.concatenate(_unpack_bf16_pair(packed), axis=0)
    act = (_gelu(s) * route).astype(BF16)
    o_ref[...] += jnp.dot(act, v_ref[...], preferred_element_type=F32)

    @pl.when(e == pl.num_programs(1) - 1)
    def _():
        o_ref[...] = _layer_norm(DN_ALPHA * h_ref[...] + o_ref[...], g_ref[...], b_ref[...])


def _peer_expert_call(hb, h, i1, i2, gate, u, v, g, b, tm, te):
    t = hb.shape[0]
    nj = PEER_TOPK * PEER_HEADS
    assert tm % (2 * BUILD_UNROLL) == 0
    stride = tm // 2 + SUBLANES
    rmap = lambda i, e: (i, 0)
    emap = lambda i, e: (e, 0)
    cmap = lambda i, e: (0, 0)
    once = dict(pipeline_mode=pl.Buffered(1))
    return pl.pallas_call(
        functools.partial(_peer_expert_kernel, tm=tm, te=te, stride=stride),
        grid=(t // tm, PEER_EXPERTS // te),
        in_specs=[pl.BlockSpec((tm, D_MODEL), rmap),
                  pl.BlockSpec((tm, D_MODEL), rmap, **once),
                  pl.BlockSpec((tm, nj), rmap),
                  pl.BlockSpec((tm, nj), rmap),
                  pl.BlockSpec((tm, nj), rmap),
                  pl.BlockSpec((te, D_MODEL), emap),
                  pl.BlockSpec((te, D_MODEL), emap),
                  pl.BlockSpec((1, D_MODEL), cmap),
                  pl.BlockSpec((1, D_MODEL), cmap)],
        out_specs=pl.BlockSpec((tm, D_MODEL), rmap),
        out_shape=jax.ShapeDtypeStruct((t, D_MODEL), F32),
        scratch_shapes=[pltpu.VMEM((PEER_NKEYS * stride, PEER_NKEYS), jnp.uint32)],
        compiler_params=_cparams(("parallel", "arbitrary"), PEER_VMEM_LIMIT),
        name="peer_experts",
    )(hb, h, i1, i2, gate, u, v, g, b)


def kernel(x, meta, ln0_g, ln0_b, w_in, ssd_conv_w, ssd_conv_b, ssd_dt_bias, ssd_a_log, ssd_d,
           ssd_norm_w, ssd_out, cf_dw_w, cf_dw_b, cf_ln_g, cf_ln_b, cf_out, w_o, ln1_g, ln1_b,
           peer_wq, peer_keys, peer_u, peer_v, ln2_g, ln2_b):
    bsz, seq, d = x.shape
    assert d == D_MODEL and seq % CHUNK == 0
    assert w_in.shape[0] == 1, "single layer"
    t = bsz * seq
    tp = t + bsz * CHUNK
    ln_rows = _divisor_tile(t, 512, 2 * CHUNK)
    assert bsz * CHUNK <= ln_rows

    rows = _divisor_tile(seq, 256, CHUNK)
    tm_proj = _divisor_tile(tp, 1664, SUBLANES)
    tn_proj = 1024
    tm_merge = _divisor_tile(t, 256, SUBLANES)
    tq = _divisor_tile(t, 512, LANES)
    tm_peer = _divisor_tile(t, 512, 2 * BUILD_UNROLL)

    h0, h0b = _ln0_call(x.reshape(t, d), meta, ln0_g, ln0_b, tp, ln_rows)

    w = w_in[0].astype(BF16)
    wcv = w[:, OFF_DT:OFF_DT + CF_WIDTH]
    wcg = w[:, OFF_DT + CF_WIDTH:OFF_CF]
    wgate = w[:, OFF_CF:]
    pad_heads = lambda a: jnp.pad(a.astype(F32), (0, LANES - SSD_HEADS)).reshape(1, LANES)
    head_expand = jnp.tile(jnp.arange(SSD_D_INNER)[None, :] // SSD_HEADDIM
                           == jnp.arange(LANES)[:, None], (3, 1)).astype(BF16)

    zs = _proj_call(h0b, [w], _silu, BF16, tm_proj, tn_proj, name="proj_z", window=(0, OFF_Z))
    xbc = _proj_call(h0b, [w], lambda a: a, BF16, tm_proj, tn_proj, name="proj_xbc",
                     window=(OFF_Z, SSD_XBC))
    dt = _proj_call(h0b, [w], _softplus, F32, tm_proj, LANES, bias=pad_heads(ssd_dt_bias[0]),
                    name="proj_dt", window=(OFF_XBC, LANES))
    gates = _proj_call(h0b, [wgate], jax.nn.sigmoid, BF16, tm_proj, tn_proj, name="proj_gate")

    yn = _ssd_call(xbc, zs, dt, ssd_conv_w[0], ssd_conv_b[0].reshape(1, -1), pad_heads(ssd_a_log[0]),
                   jnp.repeat(ssd_d[0].astype(F32), SSD_HEADDIM).reshape(1, -1),
                   ssd_norm_w[0].reshape(1, -1), head_expand, bsz, seq, rows)
    c2 = _conf_call(h0b, wcv, wcg, cf_dw_w[0], cf_dw_b[0].reshape(1, -1), cf_ln_g[0].reshape(1, -1),
                    cf_ln_b[0].reshape(1, -1), bsz, seq, rows)
    h1, h1b = _merge_call(yn, c2, gates, h0, ssd_out[0].astype(BF16), cf_out[0].astype(BF16),
                          w_o[0].astype(BF16), ln1_g[0].reshape(1, -1), ln1_b[0].reshape(1, -1),
                          tm_merge)

    i1, i2, gate = _peer_query_call(h1b, peer_wq[0].astype(BF16), peer_keys[0].astype(BF16), tq)
    out = _peer_expert_call(h1b, h1, i1, i2, gate, peer_u[0].astype(BF16), peer_v[0].astype(BF16),
                            ln2_g[0].reshape(1, -1), ln2_b[0].reshape(1, -1), tm_peer, 1024)
    return out.reshape(bsz, seq, d)
```

```python
import functools
import math

import jax
import jax.numpy as jnp
from jax import lax
from jax.experimental import pallas as pl
from jax.experimental.pallas import tpu as pltpu

F32 = jnp.float32
BF16 = jnp.bfloat16

D_MODEL = 2048
CHUNK = 64
N_META = 16
PAD_ROWS = CHUNK - N_META
SSD_D_INNER = 2048
SSD_HEADDIM = 64
SSD_HEADS = 32
SSD_GROUPS = 8
SSD_STATE = 128
SSD_CONV = 4
SSD_XBC = SSD_D_INNER + 2 * SSD_GROUPS * SSD_STATE
GROUP_COLS = SSD_D_INNER // SSD_GROUPS
HEADS_PER_GROUP = SSD_HEADS // SSD_GROUPS
CF_WIDTH = 2048
CF_CONV = 31
CF_HALO = 32
PEER_HEADS = 8
PEER_NKEYS = 128
PEER_EXPERTS = PEER_NKEYS * PEER_NKEYS
PEER_DKEY = 256
PEER_TOPK = 16
OFF_Z = SSD_D_INNER
OFF_XBC = OFF_Z + SSD_XBC
OFF_DT = OFF_XBC + SSD_HEADS
OFF_CF = OFF_DT + 2 * CF_WIDTH
LN_EPS = 1e-5
DN_ALPHA = 2.0 ** 0.25
LANES = 128
SUBLANES = 8
VMEM_LIMIT = 56 * 1024 * 1024
PEER_VMEM_LIMIT = 60 * 1024 * 1024

_CAND = [(a, b) for a in range(PEER_TOPK) for b in range(PEER_TOPK)
         if (a + 1) * (b + 1) <= PEER_TOPK]


def _cparams(sem, vmem_limit=VMEM_LIMIT):
    return pltpu.CompilerParams(dimension_semantics=sem, vmem_limit_bytes=vmem_limit)


def _resident(shape):
    nd = len(shape)
    return pl.BlockSpec(shape, lambda *_: (0,) * nd, pipeline_mode=pl.Buffered(1))


def _divisor_tile(n, target, mult):
    best = None
    for t in range(mult, min(n, target) + 1, mult):
        if n % t == 0:
            best = t
    assert best is not None, (n, target, mult)
    return best


def _layer_norm(x, g, b):
    mu = jnp.mean(x, axis=-1, keepdims=True)
    xc = x - mu
    var = jnp.mean(xc * xc, axis=-1, keepdims=True)
    return xc * lax.rsqrt(var + LN_EPS) * g + b


def _silu(x):
    return x * jax.nn.sigmoid(x)


def _ln0_kernel(x_ref, meta_ref, g_ref, b_ref, o_ref, ob_ref, *, n_token_tiles):
    i = pl.program_id(0)

    def emit(rows):
        y = _layer_norm(rows, g_ref[...], b_ref[...])
        o_ref[...] = y
        ob_ref[...] = y.astype(BF16)

    @pl.when(i < n_token_tiles)
    def _():
        emit(x_ref[...])

    @pl.when(i >= n_token_tiles)
    def _():
        head = jnp.concatenate([jnp.zeros((PAD_ROWS, D_MODEL), F32), meta_ref[...]], axis=0)
        emit(jnp.concatenate([head] * (x_ref.shape[0] // CHUNK), axis=0))


def _ln0_call(x2d, meta, g, b, tp, rows):
    t, d = x2d.shape
    n_tok = t // rows
    return pl.pallas_call(
        functools.partial(_ln0_kernel, n_token_tiles=n_tok),
        grid=(pl.cdiv(tp, rows),),
        in_specs=[pl.BlockSpec((rows, d), lambda i: (jnp.minimum(i, n_tok - 1), 0)),
                  pl.BlockSpec((N_META, d), lambda i: (0, 0)),
                  pl.BlockSpec((1, d), lambda i: (0, 0)),
                  pl.BlockSpec((1, d), lambda i: (0, 0))],
        out_specs=[pl.BlockSpec((rows, d), lambda i: (i, 0)),
                   pl.BlockSpec((rows, d), lambda i: (i, 0))],
        out_shape=[jax.ShapeDtypeStruct((tp, d), F32), jax.ShapeDtypeStruct((tp, d), BF16)],
        compiler_params=_cparams(("parallel",)),
        name="ln0",
    )(x2d, meta, g.reshape(1, d), b.reshape(1, d))


def _proj_kernel(x_ref, *refs, n_w, epilogue, has_bias):
    w_refs = refs[:n_w]
    b_ref = refs[n_w] if has_bias else None
    o_ref = refs[-1]
    x = x_ref[...]
    accs = [jnp.dot(x, w[...], preferred_element_type=F32) for w in w_refs]
    if has_bias:
        accs[0] = accs[0] + b_ref[...]
    o_ref[...] = epilogue(*accs).astype(o_ref.dtype)


def _proj_call(x, ws, epilogue, out_dtype, tm, tn, bias=None, name="proj", window=None):
    tp, k = x.shape
    col0, n = (0, ws[0].shape[1]) if window is None else window
    assert col0 % tn == 0 and n % tn == 0
    blk0 = col0 // tn
    in_specs = [pl.BlockSpec((tm, k), lambda i, j: (i, 0))]
    in_specs += [pl.BlockSpec((k, tn), lambda i, j: (0, blk0 + j)) for _ in ws]
    args = [x, *ws]
    if bias is not None:
        in_specs.append(pl.BlockSpec((1, tn), lambda i, j: (0, j)))
        args.append(bias)
    return pl.pallas_call(
        functools.partial(_proj_kernel, n_w=len(ws), epilogue=epilogue, has_bias=bias is not None),
        grid=(tp // tm, n // tn),
        in_specs=in_specs,
        out_specs=pl.BlockSpec((tm, tn), lambda i, j: (i, j)),
        out_shape=jax.ShapeDtypeStruct((tp, n), out_dtype),
        compiler_params=_cparams(("parallel", "parallel")),
        name=name,
    )(*args)


def _softplus(x):
    return jnp.maximum(x, 0.0) + jnp.log1p(jnp.exp(-jnp.abs(x)))


def _split3(x):
    hi = x.astype(BF16)
    r1 = x - hi.astype(F32)
    mid = r1.astype(BF16)
    lo = (r1 - mid.astype(F32)).astype(BF16)
    return jnp.concatenate([hi, mid, lo], axis=1)


def _ssd_block(xin, dt, n, refs, y_ref):
    cw_ref, cb_ref, alog_ref, hexp_ref, ext_ref, state_ref = refs
    hist = SUBLANES
    ext_ref[hist:hist + n, :] = xin
    acc = cb_ref[...] + cw_ref[0:1, :] * ext_ref[pl.ds(hist - (SSD_CONV - 1), n), :]
    for k in range(1, SSD_CONV):
        acc = acc + cw_ref[k:k + 1, :] * ext_ref[pl.ds(hist - (SSD_CONV - 1) + k, n), :]
    ext_ref[0:hist, :] = ext_ref[n:n + hist, :]
    xc = _silu(acc)
    xs = xc[:, :SSD_D_INNER]
    bm = xc[:, SSD_D_INNER:SSD_D_INNER + SSD_GROUPS * SSD_STATE].astype(BF16)
    cm = xc[:, SSD_D_INNER + SSD_GROUPS * SSD_STATE:].astype(BF16)

    da = dt * (-jnp.exp(alog_ref[...]))
    pos = lax.broadcasted_iota(jnp.int32, (n, 1), 0) % CHUNK
    cs = da
    shift = 1
    while shift < CHUNK:
        cs = cs + jnp.where(pos >= shift, pltpu.roll(cs, shift, axis=0), 0.0)
        shift *= 2
    hexp = hexp_ref[...]
    dt_e = jnp.dot(_split3(dt), hexp, preferred_element_type=F32)
    cs_e = jnp.dot(_split3(cs), hexp, preferred_element_type=F32)
    xdt = xs * dt_e

    sub = lax.broadcasted_iota(jnp.int32, (CHUNK, SSD_D_INNER), 0)
    lane_pos = lax.broadcasted_iota(jnp.int32, (CHUNK, SSD_D_INNER), 1) % SSD_HEADDIM
    diag = sub == lane_pos
    causal = sub >= lane_pos
    blk_r = lax.broadcasted_iota(jnp.int32, (GROUP_COLS, GROUP_COLS), 0) // SSD_HEADDIM
    blk_c = lax.broadcasted_iota(jnp.int32, (GROUP_COLS, GROUP_COLS), 1) // SSD_HEADDIM
    same_head = blk_r == blk_c

    for c in range(n // CHUNK):
        sl = slice(c * CHUNK, (c + 1) * CHUNK)
        cs_c = cs_e[sl]
        cs_end = cs_c[CHUNK - 1:CHUNK, :]
        decay_out = jnp.exp(cs_end - cs_c)
        decay_chunk = jnp.exp(cs_end)
        if y_ref is not None:
            cs_row = jnp.sum(jnp.where(diag, cs_c, 0.0), axis=0, keepdims=True)
            decay_l = jnp.where(causal, jnp.exp(cs_c - cs_row), 0.0)
            decay_in = jnp.exp(cs_c)
        for g in range(SSD_GROUPS):
            gc = slice(g * GROUP_COLS, (g + 1) * GROUP_COLS)
            gn = slice(g * SSD_STATE, (g + 1) * SSD_STATE)
            bm_g = bm[sl, gn]
            x_g = xdt[sl, gc]
            st = state_ref[g]
            if y_ref is not None:
                cm_g = cm[sl, gn]
                bm_rep = jnp.concatenate([bm_g] * HEADS_PER_GROUP, axis=0)
                scores = lax.dot_general(cm_g, bm_rep, (((1,), (1,)), ((), ())),
                                         preferred_element_type=F32)
                m = (scores * decay_l[:, gc]).astype(BF16)
                x_rep = jnp.concatenate([x_g] * HEADS_PER_GROUP, axis=0)
                x_bd = jnp.where(same_head, x_rep, 0.0).astype(BF16)
                y_diag = jnp.dot(m, x_bd, preferred_element_type=F32)
                y_off = jnp.dot(cm_g, st.astype(BF16), preferred_element_type=F32) * decay_in[:, gc]
                y_ref[sl, gc] = y_diag + y_off
            xd = (x_g * decay_out[:, gc]).astype(BF16)
            upd = lax.dot_general(bm_g, xd, (((0,), (0,)), ((), ())), preferred_element_type=F32)
            state_ref[g] = st * decay_chunk[:, gc] + upd
    return xs


def _ssd_kernel(xbc_ref, zs_ref, dt_ref, xbc_head_ref, dt_head_ref, cw_ref, cb_ref, alog_ref,
                dskip_ref, nw_ref, hexp_ref, o_ref, ext_ref, state_ref, y_ref, *, rows):
    refs = (cw_ref, cb_ref, alog_ref, hexp_ref, ext_ref, state_ref)

    @pl.when(pl.program_id(1) == 0)
    def _():
        ext_ref[0:SUBLANES, :] = jnp.zeros((SUBLANES, SSD_XBC), F32)
        state_ref[...] = jnp.zeros(state_ref.shape, F32)
        meta_row = lax.broadcasted_iota(jnp.int32, (CHUNK, 1), 0) >= PAD_ROWS
        _ssd_block(jnp.where(meta_row, xbc_head_ref[...].astype(F32), 0.0),
                   jnp.where(meta_row, dt_head_ref[...], 0.0), CHUNK, refs, None)

    xs = _ssd_block(xbc_ref[...].astype(F32), dt_ref[...], rows, refs, y_ref)
    y = y_ref[...] + dskip_ref[...] * xs
    yf = y * zs_ref[...].astype(F32)
    for g in range(SSD_GROUPS):
        gc = slice(g * GROUP_COLS, (g + 1) * GROUP_COLS)
        seg = yf[:, gc]
        ms = jnp.mean(seg * seg, axis=-1, keepdims=True)
        o_ref[:, gc] = (seg * lax.rsqrt(ms + LN_EPS) * nw_ref[:, gc]).astype(o_ref.dtype)


def _ssd_call(xbc, zs, dt, cw, cb, alog, dskip, nw, hexp, bsz, seq, rows):
    nblk = seq // rows
    head0 = bsz * seq // CHUNK
    rmap = lambda b, i: (b * nblk + i, 0)
    hmap = lambda b, i: (head0 + b, 0)
    cmap = lambda b, i: (0, 0)
    return pl.pallas_call(
        functools.partial(_ssd_kernel, rows=rows),
        grid=(bsz, nblk),
        in_specs=[pl.BlockSpec((rows, SSD_XBC), rmap),
                  pl.BlockSpec((rows, SSD_D_INNER), rmap),
                  pl.BlockSpec((rows, LANES), rmap),
                  pl.BlockSpec((CHUNK, SSD_XBC), hmap),
                  pl.BlockSpec((CHUNK, LANES), hmap),
                  pl.BlockSpec((SSD_CONV, SSD_XBC), cmap),
                  pl.BlockSpec((1, SSD_XBC), cmap),
                  pl.BlockSpec((1, LANES), cmap),
                  pl.BlockSpec((1, SSD_D_INNER), cmap),
                  pl.BlockSpec((1, SSD_D_INNER), cmap),
                  pl.BlockSpec((3 * LANES, SSD_D_INNER), cmap)],
        out_specs=pl.BlockSpec((rows, SSD_D_INNER), rmap),
        out_shape=jax.ShapeDtypeStruct((bsz * seq, SSD_D_INNER), BF16),
        scratch_shapes=[pltpu.VMEM((SUBLANES + rows, SSD_XBC), F32),
                        pltpu.VMEM((SSD_GROUPS, SSD_STATE, GROUP_COLS), F32),
                        pltpu.VMEM((rows, SSD_D_INNER), F32)],
        compiler_params=_cparams(("arbitrary", "arbitrary")),
        name="ssd",
    )(xbc, zs, dt, xbc, dt, cw, cb, alog, dskip, nw, hexp)


CF_ROW_BLK = 64
CF_COL_BLK = 256


def _glu_proj(x, wv_ref, wg_ref, cols):
    val = jnp.dot(x, wv_ref[:, cols], preferred_element_type=F32)
    gate = jnp.dot(x, wg_ref[:, cols], preferred_element_type=F32)
    return val * jax.nn.sigmoid(gate)


def _conf_kernel(x0_ref, xnext_ref, xhead_ref, wv_ref, wg_ref, w_ref, b_ref, g_ref, beta_ref, o_ref,
                 ext_ref, sh_ref, acc_ref, cnext_ref, *, rows):
    col_blocks = CF_WIDTH // CF_COL_BLK

    @pl.when(pl.program_id(1) == 0)
    def _():
        meta_row = lax.broadcasted_iota(jnp.int32, (CF_HALO, 1), 0) >= CF_HALO - N_META
        x_tail = xhead_ref[CHUNK - CF_HALO:CHUNK, :]
        for cb in range(col_blocks):
            cols = slice(cb * CF_COL_BLK, (cb + 1) * CF_COL_BLK)
            ext_ref[0:CF_HALO, cols] = jnp.where(meta_row, _glu_proj(x_tail, wv_ref, wg_ref, cols), 0.0)
            cnext_ref[:, cols] = _glu_proj(x0_ref[...], wv_ref, wg_ref, cols)

    ext_ref[CF_HALO:CF_HALO + rows, :] = cnext_ref[...]
    span = rows + CF_HALO - SUBLANES
    for s in range(1, SUBLANES):
        sh_ref[s - 1] = ext_ref[pl.ds(s, span), :]
    base = CF_HALO - (CF_CONV - 1)

    def col_block(cb, carry):
        cols = pl.ds(pl.multiple_of(cb * CF_COL_BLK, CF_COL_BLK), CF_COL_BLK)
        nxt = _glu_proj(xnext_ref[...], wv_ref, wg_ref, cols)
        for rb in range(rows // CF_ROW_BLK):
            r0 = rb * CF_ROW_BLK
            acc = jnp.broadcast_to(b_ref[:, cols], (CF_ROW_BLK, CF_COL_BLK))
            for k in range(CF_CONV):
                s = (base + k) % SUBLANES
                q = r0 + (base + k - s)
                if s == 0:
                    src = ext_ref[q:q + CF_ROW_BLK, cols]
                else:
                    src = sh_ref[s - 1, q:q + CF_ROW_BLK, cols]
                acc = acc + w_ref[k:k + 1, cols] * src
            acc_ref[r0:r0 + CF_ROW_BLK, cols] = acc
        cnext_ref[:, cols] = nxt
        return carry

    lax.fori_loop(0, col_blocks, col_block, 0)
    ext_ref[0:CF_HALO, :] = ext_ref[rows:rows + CF_HALO, :]
    o_ref[...] = _silu(_layer_norm(acc_ref[...], g_ref[...], beta_ref[...])).astype(o_ref.dtype)


def _conf_call(hb, wv, wg, w, b, g, beta, bsz, seq, rows):
    nblk = seq // rows
    head0 = bsz * seq // CHUNK
    rmap = lambda bb, i: (bb * nblk + i, 0)
    first = lambda bb, i: (bb * nblk, 0)
    nxt = lambda bb, i: (bb * nblk + jnp.minimum(i + 1, nblk - 1), 0)
    hmap = lambda bb, i: (head0 + bb, 0)
    cmap = lambda bb, i: (0, 0)
    return pl.pallas_call(
        functools.partial(_conf_kernel, rows=rows),
        grid=(bsz, nblk),
        in_specs=[pl.BlockSpec((rows, D_MODEL), first),
                  pl.BlockSpec((rows, D_MODEL), nxt),
                  pl.BlockSpec((CHUNK, D_MODEL), hmap),
                  _resident((D_MODEL, CF_WIDTH)),
                  _resident((D_MODEL, CF_WIDTH)),
                  pl.BlockSpec((CF_CONV, CF_WIDTH), cmap),
                  pl.BlockSpec((1, CF_WIDTH), cmap),
                  pl.BlockSpec((1, CF_WIDTH), cmap),
                  pl.BlockSpec((1, CF_WIDTH), cmap)],
        out_specs=pl.BlockSpec((rows, CF_WIDTH), rmap),
        out_shape=jax.ShapeDtypeStruct((bsz * seq, CF_WIDTH), BF16),
        scratch_shapes=[pltpu.VMEM((CF_HALO + rows, CF_WIDTH), F32),
                        pltpu.VMEM((SUBLANES - 1, CF_HALO + rows - SUBLANES, CF_WIDTH), F32),
                        pltpu.VMEM((rows, CF_WIDTH), F32),
                        pltpu.VMEM((rows, CF_WIDTH), F32)],
        compiler_params=_cparams(("arbitrary", "arbitrary")),
        name="conformer",
    )(hb, hb, hb, wv, wg, w, b, g, beta)


def _merge_kernel(yn_ref, c_ref, gate_ref, h_ref, wssd_ref, wcf_ref, wo_ref, g_ref, b_ref,
                  o_ref, ob_ref):
    y_ssd = jnp.dot(yn_ref[...], wssd_ref[...], preferred_element_type=F32)
    y_cf = jnp.dot(c_ref[...], wcf_ref[...], preferred_element_type=F32)
    gates = gate_ref[...].astype(F32)
    mix = gates[:, :D_MODEL] * y_ssd + gates[:, D_MODEL:] * y_cf
    m = jnp.dot(mix.astype(BF16), wo_ref[...], preferred_element_type=F32)
    h1 = _layer_norm(DN_ALPHA * h_ref[...] + m, g_ref[...], b_ref[...])
    o_ref[...] = h1
    ob_ref[...] = h1.astype(BF16)


def _merge_call(yn, c, gates, h0, wssd, wcf, wo, g, b, tm):
    t = yn.shape[0]
    rmap = lambda i: (i, 0)
    return pl.pallas_call(
        _merge_kernel,
        grid=(t // tm,),
        in_specs=[pl.BlockSpec((tm, SSD_D_INNER), rmap),
                  pl.BlockSpec((tm, CF_WIDTH), rmap),
                  pl.BlockSpec((tm, 2 * D_MODEL), rmap),
                  pl.BlockSpec((tm, D_MODEL), rmap),
                  _resident((SSD_D_INNER, D_MODEL)),
                  _resident((CF_WIDTH, D_MODEL)),
                  _resident((D_MODEL, D_MODEL)),
                  _resident((1, D_MODEL)),
                  _resident((1, D_MODEL))],
        out_specs=[pl.BlockSpec((tm, D_MODEL), rmap), pl.BlockSpec((tm, D_MODEL), rmap)],
        out_shape=[jax.ShapeDtypeStruct((t, D_MODEL), F32),
                   jax.ShapeDtypeStruct((t, D_MODEL), BF16)],
        compiler_params=_cparams(("parallel",)),
        name="merge",
    )(yn, c, gates, h0, wssd, wcf, wo, g, b)


def _slab(ref, j):
    if isinstance(j, int):
        return ref.at[:, j * PEER_HEADS:(j + 1) * PEER_HEADS, :]
    return ref.at[:, pl.ds(pl.multiple_of(j * PEER_HEADS, PEER_HEADS), PEER_HEADS), :]


def _top16_major(s_ref, ix_ref, val_ref, idx_ref, n, ids=None):
    ids = list(range(n)) if ids is None else ids
    assert n % 2 == 0 and all(x < y for x, y in zip(ids, ids[1:]))
    none = ids[-1] + 1
    neg = jnp.float32(-jnp.inf)
    half = n // 2
    m0 = None
    for p in range(half):
        a = _slab(s_ref, p)[...]
        b = _slab(s_ref, p + half)[...]
        swap = b > a
        front = jnp.where(swap, b, a)
        _slab(s_ref, p)[...] = front
        _slab(s_ref, p + half)[...] = jnp.where(swap, a, b)
        _slab(ix_ref, p)[...] = jnp.where(swap, ids[p + half], ids[p])
        _slab(ix_ref, p + half)[...] = jnp.where(swap, ids[p], ids[p + half])
        m0 = front if m0 is None else jnp.maximum(m0, front)

    def body(it, m):
        sel = jnp.full(m.shape, none, jnp.int32)
        for p in range(half):
            sel = jnp.minimum(sel, jnp.where(_slab(s_ref, p)[...] == m, _slab(ix_ref, p)[...], none))
        _slab(val_ref, it)[...] = m
        _slab(idx_ref, it)[...] = sel
        nxt = jnp.full(m.shape, neg, F32)
        for p in range(half):
            hit = _slab(ix_ref, p)[...] == sel
            back = _slab(s_ref, p + half)[...]
            front = jnp.where(hit, back, _slab(s_ref, p)[...])
            _slab(s_ref, p)[...] = front
            _slab(s_ref, p + half)[...] = jnp.where(hit, neg, back)
            _slab(ix_ref, p)[...] = jnp.where(hit, _slab(ix_ref, p + half)[...], _slab(ix_ref, p)[...])
            nxt = jnp.maximum(nxt, front)
        return nxt

    lax.fori_loop(0, PEER_TOPK, body, m0)


def _peer_query_kernel(hb_ref, wq_ref, keys_ref, i1_ref, i2_ref, gate_ref,
                       s_ref, ix_ref, v1_ref, x1_ref, v2_ref, x2_ref, cand_ref, cix_ref, bs_ref, sel_ref,
                       e1_ref, e2_ref):
    tq = hb_ref.shape[0]
    lane_blocks = tq // LANES
    q = jnp.dot(hb_ref[...], wq_ref[...], preferred_element_type=F32).astype(BF16)
    half_cols = PEER_DKEY // 2
    for half, (v_ref, x_ref) in enumerate(((v1_ref, x1_ref), (v2_ref, x2_ref))):
        for h in range(PEER_HEADS):
            c0 = h * PEER_DKEY + half * half_cols
            scores = lax.dot_general(keys_ref[half, h], q[:, c0:c0 + half_cols],
                                     (((1,), (1,)), ((), ())), preferred_element_type=F32)
            for lb in range(lane_blocks):
                s_ref[lb, pl.ds(h, PEER_NKEYS, stride=PEER_HEADS), :] = (
                    scores[:, lb * LANES:(lb + 1) * LANES])
        _top16_major(s_ref, ix_ref, v_ref, x_ref, PEER_NKEYS)

    for ci, (a, b) in enumerate(_CAND):
        _slab(cand_ref, ci)[...] = _slab(v1_ref, a)[...] + _slab(v2_ref, b)[...]
    _top16_major(cand_ref, cix_ref, bs_ref, sel_ref, len(_CAND),
                 ids=[a * PEER_TOPK + b for a, b in _CAND])

    def key_ids(it, carry):
        flat = _slab(sel_ref, it)[...]
        k1 = lax.shift_right_logical(flat, int(math.log2(PEER_TOPK)))
        k2 = flat & (PEER_TOPK - 1)
        e1 = jnp.zeros(flat.shape, jnp.int32)
        e2 = jnp.zeros(flat.shape, jnp.int32)
        for r in range(PEER_TOPK):
            e1 = jnp.where(k1 == r, _slab(x1_ref, r)[...], e1)
            e2 = jnp.where(k2 == r, _slab(x2_ref, r)[...], e2)
        _slab(e1_ref, it)[...] = e1
        _slab(e2_ref, it)[...] = e2
        return carry

    lax.fori_loop(0, PEER_TOPK, key_ids, 0)

    for lb in range(lane_blocks):
        bs = bs_ref[lb].reshape(PEER_TOPK, PEER_HEADS, LANES)
        ex = jnp.exp(bs - bs[0:1])
        gate = ex / jnp.sum(ex, axis=0, keepdims=True)
        tok = slice(lb * LANES, (lb + 1) * LANES)
        gate_ref[tok, :] = gate.reshape(PEER_TOPK * PEER_HEADS, LANES).T
        i1_ref[tok, :] = e1_ref[lb].T
        i2_ref[tok, :] = e2_ref[lb].T


def _peer_query_call(hb, wq, keys, tq):
    t = hb.shape[0]
    nj = PEER_TOPK * PEER_HEADS
    rmap = lambda i: (i, 0)
    slabs = lambda n, dt: pltpu.VMEM((tq // LANES, n * PEER_HEADS, LANES), dt)
    return pl.pallas_call(
        _peer_query_kernel,
        grid=(t // tq,),
        in_specs=[pl.BlockSpec((tq, D_MODEL), rmap),
                  _resident((D_MODEL, PEER_HEADS * PEER_DKEY)),
                  _resident((2, PEER_HEADS, PEER_NKEYS, PEER_DKEY // 2))],
        out_specs=[pl.BlockSpec((tq, nj), rmap)] * 3,
        out_shape=[jax.ShapeDtypeStruct((t, nj), jnp.int32),
                   jax.ShapeDtypeStruct((t, nj), jnp.int32),
                   jax.ShapeDtypeStruct((t, nj), F32)],
        scratch_shapes=[slabs(PEER_NKEYS, F32), slabs(PEER_NKEYS, jnp.int32),
                        slabs(PEER_TOPK, F32), slabs(PEER_TOPK, jnp.int32),
                        slabs(PEER_TOPK, F32), slabs(PEER_TOPK, jnp.int32),
                        slabs(len(_CAND), F32), slabs(len(_CAND), jnp.int32),
                        slabs(PEER_TOPK, F32), slabs(PEER_TOPK, jnp.int32),
                        slabs(PEER_TOPK, jnp.int32), slabs(PEER_TOPK, jnp.int32)],
        compiler_params=_cparams(("parallel",)),
        name="peer_query",
    )(hb, wq, keys)


BUILD_UNROLL = 32


def _gelu(x):
    return 0.5 * x * (1.0 + lax.erf(x * (1.0 / math.sqrt(2.0))))


def _pack_bf16_pair(a, b):
    bits = lambda v: lax.bitcast_convert_type(v.astype(BF16).astype(F32), jnp.uint32)
    return bits(a) | (bits(b) >> 16)


def _unpack_bf16_pair(w):
    return (lax.bitcast_convert_type(w & jnp.uint32(0xFFFF0000), F32),
            lax.bitcast_convert_type(w << 16, F32))


def _route_rows(i1_row, i2_row, gate_row, key_iota):
    a_t = jnp.where(key_iota == i1_row, gate_row, 0.0).astype(BF16)
    b_t = jnp.where(key_iota == i2_row, 1.0, 0.0).astype(BF16)
    return lax.dot_general(a_t, b_t, (((1,), (1,)), ((), ())), preferred_element_type=F32)


def _peer_expert_kernel(hb_ref, h_ref, i1_ref, i2_ref, gate_ref, u_ref, v_ref, g_ref, b_ref,
                        o_ref, gs_ref, *, tm, te, stride):
    e = pl.program_id(1)
    half = tm // 2

    @pl.when(e == 0)
    def _():
        o_ref[...] = jnp.zeros(o_ref.shape, F32)
        key_iota = lax.broadcasted_iota(jnp.int32, (PEER_NKEYS, PEER_NKEYS), 0)

        def build(blk, carry):
            lo = pl.multiple_of(blk * BUILD_UNROLL, BUILD_UNROLL)
            hi = pl.multiple_of(half + blk * BUILD_UNROLL, BUILD_UNROLL)
            rows = [(r[pl.ds(lo, BUILD_UNROLL), :], r[pl.ds(hi, BUILD_UNROLL), :])
                    for r in (i1_ref, i2_ref, gate_ref)]
            for k in range(BUILD_UNROLL):
                g0 = _route_rows(rows[0][0][k:k + 1], rows[1][0][k:k + 1], rows[2][0][k:k + 1], key_iota)
                g1 = _route_rows(rows[0][1][k:k + 1], rows[1][1][k:k + 1], rows[2][1][k:k + 1], key_iota)
                gs_ref[pl.ds(lo + k, PEER_NKEYS, stride=stride), :] = _pack_bf16_pair(g0, g1)
            return carry

        lax.fori_loop(0, half // BUILD_UNROLL, build, 0)

    s = lax.dot_general(hb_ref[...], u_ref[...], (((1,), (1,)), ((), ())),
                        preferred_element_type=F32)
    slabs = te // PEER_NKEYS
    packed = jnp.concatenate(
        [gs_ref[pl.ds(pl.multiple_of((e * slabs + r) * stride, SUBLANES), half), :]
         for r in range(slabs)], axis=1)
    route = jnp.concatenate(_unpack_bf16_pair(packed), axis=0)
    act = (_gelu(s) * route).astype(BF16)
    o_ref[...] += jnp.dot(act, v_ref[...], preferred_element_type=F32)

    @pl.when(e == pl.num_programs(1) - 1)
    def _():
        o_ref[...] = _layer_norm(DN_ALPHA * h_ref[...] + o_ref[...], g_ref[...], b_ref[...])


def _peer_expert_call(hb, h, i1, i2, gate, u, v, g, b, tm, te):
    t = hb.shape[0]
    nj = PEER_TOPK * PEER_HEADS
    assert tm % (2 * BUILD_UNROLL) == 0
    stride = tm // 2 + SUBLANES
    rmap = lambda i, e: (i, 0)
    emap = lambda i, e: (e, 0)
    cmap = lambda i, e: (0, 0)
    once = dict(pipeline_mode=pl.Buffered(1))
    return pl.pallas_call(
        functools.partial(_peer_expert_kernel, tm=tm, te=te, stride=stride),
        grid=(t // tm, PEER_EXPERTS // te),
        in_specs=[pl.BlockSpec((tm, D_MODEL), rmap),
                  pl.BlockSpec((tm, D_MODEL), rmap),
                  pl.BlockSpec((tm, nj), rmap, **once),
                  pl.BlockSpec((tm, nj), rmap, **once),
                  pl.BlockSpec((tm, nj), rmap, **once),
                  pl.BlockSpec((te, D_MODEL), emap),
                  pl.BlockSpec((te, D_MODEL), emap),
                  pl.BlockSpec((1, D_MODEL), cmap),
                  pl.BlockSpec((1, D_MODEL), cmap)],
        out_specs=pl.BlockSpec((tm, D_MODEL), rmap),
        out_shape=jax.ShapeDtypeStruct((t, D_MODEL), F32),
        scratch_shapes=[pltpu.VMEM((PEER_NKEYS * stride, PEER_NKEYS), jnp.uint32)],
        compiler_params=_cparams(("parallel", "arbitrary"), PEER_VMEM_LIMIT),
        name="peer_experts",
    )(hb, h, i1, i2, gate, u, v, g, b)


def kernel(x, meta, ln0_g, ln0_b, w_in, ssd_conv_w, ssd_conv_b, ssd_dt_bias, ssd_a_log, ssd_d,
           ssd_norm_w, ssd_out, cf_dw_w, cf_dw_b, cf_ln_g, cf_ln_b, cf_out, w_o, ln1_g, ln1_b,
           peer_wq, peer_keys, peer_u, peer_v, ln2_g, ln2_b):
    bsz, seq, d = x.shape
    assert d == D_MODEL and seq % CHUNK == 0
    assert w_in.shape[0] == 1, "single layer"
    t = bsz * seq
    tp = t + bsz * CHUNK
    ln_rows = _divisor_tile(t, 512, 2 * CHUNK)
    assert bsz * CHUNK <= ln_rows

    rows = _divisor_tile(seq, 256, CHUNK)
    tm_proj = _divisor_tile(tp, 1664, SUBLANES)
    tn_proj = 1024
    tm_merge = _divisor_tile(t, 256, SUBLANES)
    tq = _divisor_tile(t, 512, LANES)
    tm_peer = _divisor_tile(t, 512, 2 * BUILD_UNROLL)

    h0, h0b = _ln0_call(x.reshape(t, d), meta, ln0_g, ln0_b, tp, ln_rows)

    w = w_in[0].astype(BF16)
    wcv = w[:, OFF_DT:OFF_DT + CF_WIDTH]
    wcg = w[:, OFF_DT + CF_WIDTH:OFF_CF]
    wgate = w[:, OFF_CF:]
    pad_heads = lambda a: jnp.pad(a.astype(F32), (0, LANES - SSD_HEADS)).reshape(1, LANES)
    head_expand = jnp.tile(jnp.arange(SSD_D_INNER)[None, :] // SSD_HEADDIM
                           == jnp.arange(LANES)[:, None], (3, 1)).astype(BF16)

    zs = _proj_call(h0b, [w], _silu, BF16, tm_proj, tn_proj, name="proj_z", window=(0, OFF_Z))
    xbc = _proj_call(h0b, [w], lambda a: a, BF16, tm_proj, tn_proj, name="proj_xbc",
                     window=(OFF_Z, SSD_XBC))
    dt = _proj_call(h0b, [w], _softplus, F32, tm_proj, LANES, bias=pad_heads(ssd_dt_bias[0]),
                    name="proj_dt", window=(OFF_XBC, LANES))
    gates = _proj_call(h0b, [wgate], jax.nn.sigmoid, BF16, tm_proj, tn_proj, name="proj_gate")

    yn = _ssd_call(xbc, zs, dt, ssd_conv_w[0], ssd_conv_b[0].reshape(1, -1), pad_heads(ssd_a_log[0]),
                   jnp.repeat(ssd_d[0].astype(F32), SSD_HEADDIM).reshape(1, -1),
                   ssd_norm_w[0].reshape(1, -1), head_expand, bsz, seq, rows)
    c2 = _conf_call(h0b, wcv, wcg, cf_dw_w[0], cf_dw_b[0].reshape(1, -1), cf_ln_g[0].reshape(1, -1),
                    cf_ln_b[0].reshape(1, -1), bsz, seq, rows)
    h1, h1b = _merge_call(yn, c2, gates, h0, ssd_out[0].astype(BF16), cf_out[0].astype(BF16),
                          w_o[0].astype(BF16), ln1_g[0].reshape(1, -1), ln1_b[0].reshape(1, -1),
                          tm_merge)

    i1, i2, gate = _peer_query_call(h1b, peer_wq[0].astype(BF16), peer_keys[0].astype(BF16), tq)
    out = _peer_expert_call(h1b, h1, i1, i2, gate, peer_u[0].astype(BF16), peer_v[0].astype(BF16),
                            ln2_g[0].reshape(1, -1), ln2_b[0].reshape(1, -1), tm_peer, 1024)
    return out.reshape(bsz, seq, d)
```
